```python
import jax, jax.numpy as jnp
from jax import lax
import numpy as np

D_MODEL = 1024
BATCH = 8
SEQ = 2048
DEPTH = 4

D_MIX = D_MODEL
SSD_HEAD_DIM = 64
SSD_WIDTH = D_MIX // 2
SSD_HEADS = SSD_WIDTH // SSD_HEAD_DIM
SSD_GROUPS = 2
SSD_HEADS_PER_GROUP = SSD_HEADS // SSD_GROUPS
D_STATE = 128
CONV_WIDTH = 4
CHUNK = 128
CONV_DIM = SSD_WIDTH + 2 * SSD_GROUPS * D_STATE
SB_HEAD_DIM = 64
SB_WIDTH = D_MIX // 4
SB_HEADS = SB_WIDTH // SB_HEAD_DIM
Q_BLOCK = 128
POOL_WINDOWS = (2, 4, 8, 16)
POOL_GROUPS = len(POOL_WINDOWS)
POOL_WIDTH = D_MIX - SSD_WIDTH - SB_WIDTH
POOL_GROUP_DIM = POOL_WIDTH // POOL_GROUPS
D_IN_PROJ = SSD_WIDTH + CONV_DIM + SSD_HEADS + 3 * SB_WIDTH + POOL_WIDTH
SPLIT_POINTS = (SSD_WIDTH,
                SSD_WIDTH + CONV_DIM,
                SSD_WIDTH + CONV_DIM + SSD_HEADS,
                SSD_WIDTH + CONV_DIM + SSD_HEADS + 3 * SB_WIDTH)
D_FF = -(-8 * D_MODEL // (3 * 256)) * 256
EPS = 1e-6

kernel_name = 'hybrid_ssd_stickbreak_pool_trunk'


def rmsnorm(x, w):
    xf = x.astype(jnp.float32)
    y = xf * lax.rsqrt(jnp.mean(xf * xf, axis=-1, keepdims=True) + EPS)
    return (y * w.astype(jnp.float32)).astype(x.dtype)


def causal_depthwise_conv(u, w, b):
    out = lax.conv_general_dilated(
        u, w[:, None, :].astype(u.dtype), window_strides=(1,),
        padding=[(CONV_WIDTH - 1, 0)],
        dimension_numbers=('NWC', 'WIO', 'NWC'),
        feature_group_count=u.shape[-1])
    return out + b.astype(u.dtype)


def ssd_mixer(z, xbc, dt_raw, conv_w, conv_b, dt_bias, a_log, d_skip, norm_w):
    f32 = jnp.float32
    bsz, seqlen, _ = xbc.shape
    nc = seqlen // CHUNK
    G, K, P, N, L = SSD_GROUPS, SSD_HEADS_PER_GROUP, SSD_HEAD_DIM, D_STATE, CHUNK
    xbc = jax.nn.silu(causal_depthwise_conv(xbc, conv_w, conv_b))
    xs, bm, cm = jnp.split(xbc, [SSD_WIDTH, SSD_WIDTH + SSD_GROUPS * D_STATE], axis=-1)
    dt = jax.nn.softplus(dt_raw.astype(f32) + dt_bias.astype(f32))
    a = -jnp.exp(a_log.astype(f32))
    xh = xs.astype(f32).reshape(bsz, nc, L, G, K, P)
    dtc = dt.reshape(bsz, nc, L, G, K)
    X = xh * dtc[..., None]
    Bc = bm.astype(f32).reshape(bsz, nc, L, G, N)
    Cc = cm.astype(f32).reshape(bsz, nc, L, G, N)
    dA = jnp.transpose(dtc * a.reshape(G, K), (0, 1, 3, 4, 2))
    acum = jnp.cumsum(dA, axis=-1)
    causal = jnp.tril(jnp.ones((L, L), dtype=bool))
    seg = jnp.where(causal, acum[..., :, None] - acum[..., None, :], -jnp.inf)
    decay_in = jnp.exp(seg)
    cb = jnp.einsum('bclgn,bcsgn->bcgls', Cc, Bc)
    y_diag = jnp.einsum('bcgls,bcgkls,bcsgkp->bclgkp', cb, decay_in, X)
    decay_to_end = jnp.exp(acum[..., -1:] - acum)
    chunk_states = jnp.einsum('bclgn,bcgkl,bclgkp->bcgkpn', Bc, decay_to_end, X)
    chunk_decay = jnp.exp(acum[..., -1])

    def step(state, inp):
        st, dec = inp
        return state * dec[..., None, None] + st, state

    init = jnp.zeros((bsz, G, K, P, N), f32)
    _, prev = lax.scan(step, init, (jnp.moveaxis(chunk_states, 1, 0), jnp.moveaxis(chunk_decay, 1, 0)))
    prev = jnp.moveaxis(prev, 0, 1)
    y_off = jnp.einsum('bclgn,bcgkpn,bcgkl->bclgkp', Cc, prev, jnp.exp(acum))
    y = y_diag + y_off + xh * d_skip.astype(f32).reshape(G, K)[:, :, None]
    y = y.reshape(bsz, seqlen, SSD_WIDTH) * jax.nn.silu(z.astype(f32))
    return rmsnorm(y, norm_w).astype(z.dtype)


def stick_breaking_attention(q, k, v):
    f32 = jnp.float32
    bsz, seqlen, _ = q.shape
    qh = q.astype(f32).reshape(bsz, seqlen, SB_HEADS, SB_HEAD_DIM) * (SB_HEAD_DIM ** -0.5)
    kh = k.astype(f32).reshape(bsz, seqlen, SB_HEADS, SB_HEAD_DIM)
    vh = v.astype(f32).reshape(bsz, seqlen, SB_HEADS, SB_HEAD_DIM)
    outs = []
    for start in range(0, seqlen, Q_BLOCK):
        end = start + Q_BLOCK
        logits = jnp.einsum('bthd,bshd->bhts', qh[:, start:end], kh[:, :end])
        before = jnp.arange(end)[None, :] < jnp.arange(start, end)[:, None]
        log_keep = jnp.where(before, jax.nn.log_sigmoid(-logits), 0.0)
        log_keep_after = lax.cumsum(log_keep, axis=3, reverse=True) - log_keep
        w = jnp.where(before, jnp.exp(jax.nn.log_sigmoid(logits) + log_keep_after), 0.0)
        outs.append(jnp.einsum('bhts,bshd->bthd', w, vh[:, :end]))
    o = jnp.concatenate(outs, axis=1)
    return o.reshape(bsz, seqlen, SB_WIDTH).astype(q.dtype)


def multiscale_pool(p, pool_w, pool_b, pool_scale):
    f32 = jnp.float32
    bsz, seqlen, _ = p.shape
    groups = p.astype(f32).reshape(bsz, seqlen, POOL_GROUPS, POOL_GROUP_DIM)
    csum = jnp.pad(jnp.cumsum(groups, axis=1), ((0, 0), (1, 0), (0, 0), (0, 0)))
    pos = jnp.arange(seqlen)
    pooled = []
    for gi, win in enumerate(POOL_WINDOWS):
        cg = csum[:, :, gi]
        lo = jnp.maximum(pos + 1 - win, 0)
        wsum = cg[:, 1:] - cg[:, lo]
        count = jnp.minimum(pos + 1, win).astype(f32)
        pooled.append(wsum / count[None, :, None] - groups[:, :, gi])
    pooled = jnp.stack(pooled, axis=2)
    mixed = jnp.einsum('bsgc,gcd->bsgd', pooled, pool_w.astype(f32)) + pool_b.astype(f32)
    return (mixed.reshape(bsz, seqlen, POOL_WIDTH) * pool_scale.astype(f32)).astype(p.dtype)


def setup_inputs(seed: int = 0) -> dict:
    key = jax.random.key(seed)
    ks = jax.random.split(key, 20)
    f32 = jnp.float32
    nrm = lambda k, shape, s: jax.random.normal(k, shape, f32) * s
    dt0 = jnp.exp(jax.random.uniform(ks[5], (DEPTH, SSD_HEADS), f32, np.log(1e-3), np.log(1e-1)))
    return {
        'x': jax.random.normal(ks[0], (BATCH, SEQ, D_MODEL), f32),
        'norm1_w': 1.0 + nrm(ks[1], (DEPTH, D_MODEL), 0.02),
        'w_in': nrm(ks[2], (DEPTH, D_MODEL, D_IN_PROJ), D_MODEL ** -0.5),
        'conv_w': nrm(ks[3], (DEPTH, CONV_WIDTH, CONV_DIM), CONV_WIDTH ** -0.5),
        'conv_b': nrm(ks[4], (DEPTH, CONV_DIM), 0.02),
        'dt_bias': dt0 + jnp.log(-jnp.expm1(-dt0)),
        'a_log': jnp.log(jax.random.uniform(ks[6], (DEPTH, SSD_HEADS), f32, 1.0, 16.0)),
        'd_skip': 1.0 + nrm(ks[7], (DEPTH, SSD_HEADS), 0.1),
        'ssd_norm_w': 1.0 + nrm(ks[8], (DEPTH, SSD_WIDTH), 0.02),
        'pool_w': nrm(ks[9], (DEPTH, POOL_GROUPS, POOL_GROUP_DIM, POOL_GROUP_DIM), POOL_GROUP_DIM ** -0.5),
        'pool_b': nrm(ks[10], (DEPTH, POOL_GROUPS, POOL_GROUP_DIM), 0.02),
        'pool_scale': 1.0 + nrm(ks[11], (DEPTH, POOL_WIDTH), 0.1),
        'w_out': nrm(ks[12], (DEPTH, D_MIX, D_MODEL), D_MIX ** -0.5),
        'norm2_w': 1.0 + nrm(ks[13], (DEPTH, D_MODEL), 0.02),
        'w_gate': nrm(ks[14], (DEPTH, D_MODEL, D_FF), D_MODEL ** -0.5),
        'w_up': nrm(ks[15], (DEPTH, D_MODEL, D_FF), D_MODEL ** -0.5),
        'w_down': nrm(ks[16], (DEPTH, D_FF, D_MODEL), D_FF ** -0.5),
        'final_norm_w': 1.0 + nrm(ks[17], (D_MODEL,), 0.02),
    }


def reference(x, norm1_w, w_in, conv_w, conv_b, dt_bias, a_log, d_skip, ssd_norm_w,
              pool_w, pool_b, pool_scale, w_out, norm2_w, w_gate, w_up, w_down, final_norm_w):
    for layer in range(DEPTH):
        h = rmsnorm(x, norm1_w[layer])
        proj = h @ w_in[layer]
        z, xbc, dt_raw, qkv, p = jnp.split(proj, list(SPLIT_POINTS), axis=-1)
        q, k, v = jnp.split(qkv, 3, axis=-1)
        y_ssd = ssd_mixer(z, xbc, dt_raw, conv_w[layer], conv_b[layer], dt_bias[layer],
                          a_log[layer], d_skip[layer], ssd_norm_w[layer])
        y_sb = stick_breaking_attention(q, k, v)
        y_pool = multiscale_pool(p, pool_w[layer], pool_b[layer], pool_scale[layer])
        y = jnp.concatenate([y_ssd, y_sb, y_pool], axis=-1)
        x = x + y @ w_out[layer]
        h = rmsnorm(x, norm2_w[layer])
        x = x + (jax.nn.silu(h @ w_gate[layer]) * (h @ w_up[layer])) @ w_down[layer]
    return rmsnorm(x, final_norm_w)
```

```python
import functools

import jax
import jax.numpy as jnp
from jax import lax
from jax.experimental import pallas as pl
from jax.experimental.pallas import tpu as pltpu

F32 = jnp.float32
BF16 = jnp.bfloat16

D_MODEL = 1024
DEPTH = 4
SSD_WIDTH = 512
SSD_HEADS = 8
SSD_HEAD_DIM = 64
SSD_GROUPS = 2
D_STATE = 128
CONV_WIDTH = 4
CHUNK = 128
CONV_DIM = SSD_WIDTH + 2 * SSD_GROUPS * D_STATE
SB_WIDTH = 256
SB_HEADS = 4
SB_HEAD_DIM = 64
POOL_WINDOWS = (2, 4, 8, 16)
POOL_WIDTH = 256
POOL_GROUP_DIM = 64
D_FF = 2816
EPS = 1e-6
HEAD_DIM_SHIFT = 6

LANES = 128
SUBLANES = 8
DT_PAD = LANES
D_IN_PERM = SSD_WIDTH + CONV_DIM + 3 * SB_WIDTH + POOL_WIDTH + DT_PAD
VMEM_LIMIT = 56 * 1024 * 1024

TM_PROJ = 512
TM_FFN = 512
FF_SPLIT = 1536
SB_BLOCK = 256
POOL_HALO = 16


def _dot(a, b):
    return jnp.dot(a, b, preferred_element_type=F32)


def _softplus(x):
    return jnp.maximum(x, 0.0) + jnp.log1p(jnp.exp(-jnp.abs(x)))


def _silu(x):
    return x / (1.0 + jnp.exp(-x))


def _split_bf16(x, n):
    parts = []
    r = x
    for i in range(n):
        p = r.astype(BF16)
        parts.append(p)
        if i + 1 < n:
            r = r - p.astype(F32)
    return parts


def _dot_f32_lhs(x, rhs_exact, n):
    parts = _split_bf16(x, n)
    acc = _dot(parts[0], rhs_exact)
    for p in parts[1:]:
        acc = acc + _dot(p, rhs_exact)
    return acc


def _dot_f32_rhs(lhs_exact, x, n):
    parts = _split_bf16(x, n)
    acc = _dot(lhs_exact, parts[0])
    for p in parts[1:]:
        acc = acc + _dot(lhs_exact, p)
    return acc


def _rmsnorm(x, w):
    return x * lax.rsqrt(jnp.mean(x * x, axis=-1, keepdims=True) + EPS) * w


_PROJ_SEGS = (
    (0, SSD_WIDTH),
    (SSD_WIDTH, SSD_WIDTH + CONV_DIM),
    (SSD_WIDTH + CONV_DIM, SSD_WIDTH + CONV_DIM + 3 * SB_WIDTH),
    (SSD_WIDTH + CONV_DIM + 3 * SB_WIDTH, D_IN_PERM - DT_PAD),
    (D_IN_PERM - DT_PAD, D_IN_PERM),
)


def _norm_inproj_kernel(x_ref, nw_ref, w_ref, *out_refs):
    hb = _rmsnorm(x_ref[...], nw_ref[...]).astype(BF16)
    for ref, (a, b) in zip(out_refs, _PROJ_SEGS):
        ref[...] = _dot(hb, w_ref[:, a:b])


def _norm_inproj(x2, norm_w, w_perm):
    nt = x2.shape[0]
    grid = (nt // TM_PROJ,)
    row_spec = lambda width: pl.BlockSpec((TM_PROJ, width), lambda i: (i, 0))
    const = lambda shape: pl.BlockSpec(shape, lambda i: (0, 0), pipeline_mode=pl.Buffered(1))
    widths = [b - a for a, b in _PROJ_SEGS]
    return pl.pallas_call(
        _norm_inproj_kernel,
        grid=grid,
        in_specs=[row_spec(D_MODEL), const((1, D_MODEL)), const((D_MODEL, D_IN_PERM))],
        out_specs=[row_spec(w) for w in widths],
        out_shape=[jax.ShapeDtypeStruct((nt, w), F32) for w in widths],
        compiler_params=pltpu.CompilerParams(
            dimension_semantics=("arbitrary",), vmem_limit_bytes=VMEM_LIMIT),
        name="norm_inproj",
    )(x2, norm_w, w_perm)


def _ssd_kernel(z_ref, xbc_ref, dt_ref, convw_ref, convb_ref, dtb_ref, alog_ref, dskip_ref,
                nw_ref, o_ref, ext_ref, state_ref):
    L = CHUNK
    c = pl.program_id(1)

    @pl.when(c == 0)
    def _():
        ext_ref[0:SUBLANES, :] = jnp.zeros((SUBLANES, CONV_DIM), F32)
        state_ref[...] = jnp.zeros_like(state_ref)

    ext_ref[SUBLANES:SUBLANES + L, :] = xbc_ref[...]
    u = convb_ref[...]
    for i in range(CONV_WIDTH):
        off = SUBLANES - (CONV_WIDTH - 1) + i
        u = u + convw_ref[i:i + 1, :] * ext_ref[off:off + L, :]
    ext_ref[0:SUBLANES, :] = ext_ref[L:L + SUBLANES, :]
    act = _silu(u)
    xs = act[:, 0:SSD_WIDTH]
    xs_b = xs.astype(BF16)

    dt = _softplus(dt_ref[...] + dtb_ref[...])
    a = -jnp.exp(alog_ref[...])
    d_a = dt * a

    row = lax.broadcasted_iota(jnp.int32, (L, L), 0)
    col = lax.broadcasted_iota(jnp.int32, (L, L), 1)
    causal = col <= row
    tri_incl = causal.astype(BF16)
    acum = _dot_f32_rhs(tri_incl, d_a, 3)
    upper = (row <= col).astype(BF16)
    d_a_t = jnp.transpose(d_a)[0:2 * SUBLANES, :]
    acum_t = _dot_f32_lhs(d_a_t, upper, 3)
    dt_t = jnp.transpose(dt)[0:2 * SUBLANES, :]

    total = acum[L - 1:L, :]
    eac = jnp.exp(acum)
    f_state = dt * jnp.exp(total - acum)

    erow = lax.broadcasted_iota(jnp.int32, (LANES, SSD_WIDTH), 0)
    ecol = lax.broadcasted_iota(jnp.int32, (LANES, SSD_WIDTH), 1)
    expand = (jnp.right_shift(ecol, HEAD_DIM_SHIFT) == erow).astype(BF16)
    eac_x = _dot_f32_lhs(eac, expand, 2)
    f_state_x = _dot_f32_lhs(f_state, expand, 2)

    lane = lax.broadcasted_iota(jnp.int32, (L, LANES), 1)
    low_half = lane < SSD_HEAD_DIM

    y_parts = []
    for g in range(SSD_GROUPS):
        b_g = act[:, SSD_WIDTH + g * D_STATE:SSD_WIDTH + (g + 1) * D_STATE].astype(BF16)
        c_off = SSD_WIDTH + SSD_GROUPS * D_STATE
        c_g = act[:, c_off + g * D_STATE:c_off + (g + 1) * D_STATE].astype(BF16)
        cb = lax.dot_general(c_g, b_g, (((1,), (1,)), ((), ())), preferred_element_type=F32)
        gw = SSD_WIDTH // SSD_GROUPS
        st = state_ref[g]
        y_off = _dot(c_g, st.astype(BF16)) * eac_x[:, g * gw:(g + 1) * gw]
        for j in range(2):
            lo = g * gw + j * LANES
            x_pair = xs_b[:, lo:lo + LANES]
            y_pair = None
            for k in range(2):
                h = (lo // SSD_HEAD_DIM) + k
                seg = acum[:, h:h + 1] - acum_t[h:h + 1, :]
                decay = jnp.exp(jnp.where(causal, seg, -jnp.inf))
                m = (cb * decay * dt_t[h:h + 1, :]).astype(BF16)
                x_h = jnp.where(low_half if k == 0 else jnp.logical_not(low_half), x_pair, 0)
                contrib = _dot(m, x_h)
                y_pair = contrib if y_pair is None else y_pair + contrib
            y_parts.append(y_pair + y_off[:, j * LANES:(j + 1) * LANES])
        xd = (xs[:, g * gw:(g + 1) * gw] * f_state_x[:, g * gw:(g + 1) * gw]).astype(BF16)
        s_add = lax.dot_general(b_g, xd, (((0,), (0,)), ((), ())), preferred_element_type=F32)
        state_ref[g] = st * eac_x[L - 1:L, g * gw:(g + 1) * gw] + s_add

    y = jnp.concatenate(y_parts, axis=1) + xs * dskip_ref[...]
    y = y * _silu(z_ref[...])
    o_ref[...] = _rmsnorm(y, nw_ref[...])


def _ssd(z, xbc, dt, conv_w, conv_b, dt_bias, a_log, d_skip_x, norm_w, batch, seq):
    nt = z.shape[0]
    nc = seq // CHUNK
    row = lambda width: pl.BlockSpec((CHUNK, width), lambda b, c: (b * nc + c, 0))
    const = lambda shape: pl.BlockSpec(shape, lambda b, c: (0, 0))
    return pl.pallas_call(
        _ssd_kernel,
        grid=(batch, nc),
        in_specs=[row(SSD_WIDTH), row(CONV_DIM), row(DT_PAD),
                  const((CONV_WIDTH, CONV_DIM)), const((1, CONV_DIM)), const((1, DT_PAD)),
                  const((1, DT_PAD)), const((1, SSD_WIDTH)), const((1, SSD_WIDTH))],
        out_specs=row(SSD_WIDTH),
        out_shape=jax.ShapeDtypeStruct((nt, SSD_WIDTH), F32),
        scratch_shapes=[pltpu.VMEM((CHUNK + SUBLANES, CONV_DIM), F32),
                        pltpu.VMEM((SSD_GROUPS, D_STATE, SSD_WIDTH // SSD_GROUPS), F32)],
        compiler_params=pltpu.CompilerParams(
            dimension_semantics=("arbitrary", "arbitrary"), vmem_limit_bytes=VMEM_LIMIT),
        name="ssd_mixer",
    )(z, xbc, dt, conv_w, conv_b, dt_bias, a_log, d_skip_x, norm_w)


def _sb_kernel(q_ref, k_ref, v_ref, o_ref, kb_ref, vb_ref, acc_ref):
    T = SB_BLOCK
    qi = pl.program_id(1)

    @pl.when(qi == 0)
    def _():
        kb_ref[...] = k_ref[...].astype(BF16)
        vb_ref[...] = v_ref[...].astype(BF16)

    q = q_ref[...] * (SB_HEAD_DIM ** -0.5)
    row = lax.broadcasted_iota(jnp.int32, (T, T), 0)
    col = lax.broadcasted_iota(jnp.int32, (T, T), 1)
    before = col < row
    tri = (row > col).astype(BF16)
    head_of_lane = jnp.right_shift(col, HEAD_DIM_SHIFT)

    def scores(qh, start):
        kb = kb_ref[pl.ds(start, T), :]
        z = lax.dot_general(qh, kb, (((1,), (1,)), ((), ())), preferred_element_type=F32)
        return z, _softplus(z)

    out = jnp.zeros((T, SB_WIDTH), F32)
    for h in range(SB_HEADS):
        head_mask = head_of_lane == h
        qh = jnp.where(head_mask, q, 0.0).astype(BF16)

        start = pl.multiple_of(qi * T, T)
        z, sp = scores(qh, start)
        sp = jnp.where(before, sp, 0.0)
        cum = _dot_f32_lhs(sp, tri, 2)
        w = jnp.where(before, jnp.exp(z - sp - cum), 0.0)
        acc_ref[...] = _dot(w.astype(BF16), vb_ref[pl.ds(start, T), :])
        carry0 = jnp.sum(sp, axis=1, keepdims=True)

        def body(it, carry):
            start = pl.multiple_of((qi - 1 - it) * T, T)
            z, sp = scores(qh, start)
            cum = _dot_f32_lhs(sp, tri, 2)
            w = jnp.exp(z - sp - cum - carry)
            acc_ref[...] += _dot(w.astype(BF16), vb_ref[pl.ds(start, T), :])
            return carry + jnp.sum(sp, axis=1, keepdims=True)

        lax.fori_loop(0, qi, body, carry0)
        out = jnp.where(head_mask, acc_ref[...], out)
    o_ref[...] = out


def _sb_attention(qkv, batch, seq):
    nt = qkv.shape[0]
    nq = seq // SB_BLOCK
    return pl.pallas_call(
        _sb_kernel,
        grid=(batch, nq),
        in_specs=[pl.BlockSpec((SB_BLOCK, SB_WIDTH), lambda b, i: (b * nq + i, 0)),
                  pl.BlockSpec((seq, SB_WIDTH), lambda b, i: (b, 1)),
                  pl.BlockSpec((seq, SB_WIDTH), lambda b, i: (b, 2))],
        out_specs=pl.BlockSpec((SB_BLOCK, SB_WIDTH), lambda b, i: (b * nq + i, 0)),
        out_shape=jax.ShapeDtypeStruct((nt, SB_WIDTH), F32),
        scratch_shapes=[pltpu.VMEM((seq, SB_WIDTH), BF16), pltpu.VMEM((seq, SB_WIDTH), BF16),
                        pltpu.VMEM((SB_BLOCK, SB_WIDTH), F32)],
        compiler_params=pltpu.CompilerParams(
            dimension_semantics=("arbitrary", "arbitrary"), vmem_limit_bytes=VMEM_LIMIT),
        name="sb_attention",
    )(qkv, qkv, qkv)


def _pool_out_ffn_kernel(x_ref, yssd_ref, ysb_ref, p_ref, halo_ref, poolw_ref, poolb_ref,
                         pscale_ref, wout_ref, n2_ref, wg_ref, wu_ref, wd_ref, fn_ref,
                         o_ref, buf_a, buf_b, *, tiles_per_seq, final_norm):
    TM = TM_FFN
    i = pl.program_id(0)
    t0 = (i % tiles_per_seq) * TM

    top = SUBLANES + POOL_HALO
    p_cur = p_ref[...]
    zeros8 = jnp.zeros((SUBLANES, POOL_WIDTH), F32)
    buf_a[0:SUBLANES, :] = zeros8
    buf_b[0:SUBLANES, :] = zeros8
    buf_a[SUBLANES:top, :] = jnp.where(t0 == 0, 0.0, halo_ref[...])
    buf_a[top:top + TM, :] = p_cur
    lane = lax.broadcasted_iota(jnp.int32, (TM, POOL_WIDTH), 1)
    group = jnp.right_shift(lane, HEAD_DIM_SHIFT)
    n_rows = TM + POOL_HALO
    wsum = None
    src, dst = buf_a, buf_b
    for gi, win in enumerate(POOL_WINDOWS):
        shift = win // 2
        s = src[SUBLANES:SUBLANES + n_rows, :] + src[SUBLANES - shift:SUBLANES - shift + n_rows, :]
        dst[SUBLANES:SUBLANES + n_rows, :] = s
        tile_sum = s[POOL_HALO:, :]
        wsum = tile_sum if wsum is None else jnp.where(group >= gi, tile_sum, wsum)
        src, dst = dst, src
    pos1 = t0 + 1 + lax.broadcasted_iota(jnp.int32, (TM, POOL_WIDTH), 0)
    win_lane = jnp.left_shift(2, group)
    count = jnp.minimum(pos1, win_lane).astype(F32)
    pooled = wsum / count - p_cur
    mixed = _dot(pooled.astype(BF16), poolw_ref[...]) + poolb_ref[...]
    y_pool = mixed * pscale_ref[...]

    a0, a1 = SSD_WIDTH, SSD_WIDTH + SB_WIDTH
    o_ref[...] = (x_ref[...]
                  + _dot(yssd_ref[...].astype(BF16), wout_ref[0:a0, :])
                  + _dot(ysb_ref[...].astype(BF16), wout_ref[a0:a1, :])
                  + _dot(y_pool.astype(BF16), wout_ref[a1:, :]))

    x1 = o_ref[...]
    hb = _rmsnorm(x1, n2_ref[...]).astype(BF16)
    acc = x1
    for lo, hi in ((0, FF_SPLIT), (FF_SPLIT, D_FF)):
        gate = _dot(hb, wg_ref[:, lo:hi])
        up = _dot(hb, wu_ref[:, lo:hi])
        act = (_silu(gate) * up).astype(BF16)
        acc = acc + _dot(act, wd_ref[lo:hi, :])
    if final_norm:
        acc = _rmsnorm(acc, fn_ref[...])
    o_ref[...] = acc


def _pool_out_ffn(x2, y_ssd, y_sb, p, pool_w_bd, pool_b, pool_scale, w_out, norm2_w, w_gate,
                  w_up, w_down, final_w, seq, final_norm):
    nt = x2.shape[0]
    tiles_per_seq = seq // TM_FFN
    halo_per_tile = TM_FFN // POOL_HALO
    row = lambda width: pl.BlockSpec((TM_FFN, width), lambda i: (i, 0))
    const = lambda shape: pl.BlockSpec(shape, lambda i: (0, 0), pipeline_mode=pl.Buffered(1))
    halo = pl.BlockSpec((POOL_HALO, POOL_WIDTH),
                        lambda i: (jnp.maximum(i * halo_per_tile - 1, 0), 0))
    kern = functools.partial(_pool_out_ffn_kernel, tiles_per_seq=tiles_per_seq,
                             final_norm=final_norm)
    buf_rows = SUBLANES + POOL_HALO + TM_FFN
    return pl.pallas_call(
        kern,
        grid=(nt // TM_FFN,),
        in_specs=[row(D_MODEL), row(SSD_WIDTH), row(SB_WIDTH), row(POOL_WIDTH), halo,
                  const((POOL_WIDTH, POOL_WIDTH)), const((1, POOL_WIDTH)), const((1, POOL_WIDTH)),
                  const((D_MODEL, D_MODEL)), const((1, D_MODEL)),
                  const((D_MODEL, D_FF)), const((D_MODEL, D_FF)), const((D_FF, D_MODEL)),
                  const((1, D_MODEL))],
        out_specs=row(D_MODEL),
        out_shape=jax.ShapeDtypeStruct((nt, D_MODEL), F32),
        scratch_shapes=[pltpu.VMEM((buf_rows, POOL_WIDTH), F32),
                        pltpu.VMEM((buf_rows, POOL_WIDTH), F32)],
        compiler_params=pltpu.CompilerParams(
            dimension_semantics=("arbitrary",), vmem_limit_bytes=VMEM_LIMIT),
        name="pool_out_ffn",
    )(x2, y_ssd, y_sb, p, p, pool_w_bd, pool_b, pool_scale, w_out, norm2_w, w_gate, w_up,
      w_down, final_w)


def _permute_w_in(w_in):
    s0 = SSD_WIDTH + CONV_DIM
    s1 = s0 + SSD_HEADS
    pad = jnp.zeros((w_in.shape[0], DT_PAD - SSD_HEADS), w_in.dtype)
    return jnp.concatenate([w_in[:, :s0], w_in[:, s1:], w_in[:, s0:s1], pad], axis=1)


def _block_diag(pool_w):
    g, c, d = pool_w.shape
    out = jnp.zeros((g * c, g * d), pool_w.dtype)
    for i in range(g):
        out = out.at[i * c:(i + 1) * c, i * d:(i + 1) * d].set(pool_w[i])
    return out


def _pad_lanes(v, width):
    return jnp.pad(v, (0, width - v.shape[0])).reshape(1, width)


def kernel(x, norm1_w, w_in, conv_w, conv_b, dt_bias, a_log, d_skip, ssd_norm_w, pool_w, pool_b,
           pool_scale, w_out, norm2_w, w_gate, w_up, w_down, final_norm_w):
    batch, seq, d_model = x.shape
    x2 = x.reshape(batch * seq, d_model)
    final_w = final_norm_w.reshape(1, d_model)
    for layer in range(DEPTH):
        w_perm = _permute_w_in(w_in[layer]).astype(BF16)
        z, xbc, qkv, p, dt = _norm_inproj(x2, norm1_w[layer].reshape(1, d_model), w_perm)
        y_ssd = _ssd(z, xbc, dt, conv_w[layer], conv_b[layer].reshape(1, CONV_DIM),
                     _pad_lanes(dt_bias[layer], DT_PAD), _pad_lanes(a_log[layer], DT_PAD),
                     jnp.repeat(d_skip[layer], SSD_HEAD_DIM).reshape(1, SSD_WIDTH),
                     ssd_norm_w[layer].reshape(1, SSD_WIDTH), batch, seq)
        y_sb = _sb_attention(qkv, batch, seq)
        x2 = _pool_out_ffn(
            x2, y_ssd, y_sb, p, _block_diag(pool_w[layer]).astype(BF16),
            pool_b[layer].reshape(1, POOL_WIDTH), pool_scale[layer].reshape(1, POOL_WIDTH),
            w_out[layer].astype(BF16), norm2_w[layer].reshape(1, d_model),
            w_gate[layer].astype(BF16), w_up[layer].astype(BF16), w_down[layer].astype(BF16),
            final_w, seq, final_norm=(layer == DEPTH - 1))
    return x2.reshape(batch, seq, d_model)
```

```python
import functools

import jax
import jax.numpy as jnp
from jax import lax
from jax.experimental import pallas as pl
from jax.experimental.pallas import tpu as pltpu

F32 = jnp.float32
BF16 = jnp.bfloat16

D_MODEL = 1024
DEPTH = 4
SSD_WIDTH = 512
SSD_HEADS = 8
SSD_HEAD_DIM = 64
SSD_GROUPS = 2
D_STATE = 128
CONV_WIDTH = 4
CHUNK = 128
CONV_DIM = SSD_WIDTH + 2 * SSD_GROUPS * D_STATE
SB_WIDTH = 256
SB_HEADS = 4
SB_HEAD_DIM = 64
POOL_WINDOWS = (2, 4, 8, 16)
POOL_WIDTH = 256
POOL_GROUP_DIM = 64
D_FF = 2816
EPS = 1e-6
HEAD_DIM_SHIFT = 6
LOG2E = 1.4426950408889634

LANES = 128
SUBLANES = 8
DT_PAD = LANES
D_IN_PERM = SSD_WIDTH + CONV_DIM + 3 * SB_WIDTH + POOL_WIDTH + DT_PAD
VMEM_LIMIT = 56 * 1024 * 1024

TM_PROJ = 512
TM_FFN = 512
FF_SPLIT = 1536
SB_BLOCK = 256
POOL_HALO = 16


def _dot(a, b):
    return jnp.dot(a, b, preferred_element_type=F32)


def _softplus(x):
    return jnp.maximum(x, 0.0) + jnp.log1p(jnp.exp(-jnp.abs(x)))


def _silu(x):
    return x / (1.0 + jnp.exp(-x))


def _split_bf16(x, n):
    parts = []
    r = x
    for i in range(n):
        p = r.astype(BF16)
        parts.append(p)
        if i + 1 < n:
            r = r - p.astype(F32)
    return parts


def _dot_f32_lhs(x, rhs_exact, n):
    parts = _split_bf16(x, n)
    acc = _dot(parts[0], rhs_exact)
    for p in parts[1:]:
        acc = acc + _dot(p, rhs_exact)
    return acc


def _dot_f32_rhs(lhs_exact, x, n):
    parts = _split_bf16(x, n)
    acc = _dot(lhs_exact, parts[0])
    for p in parts[1:]:
        acc = acc + _dot(lhs_exact, p)
    return acc


def _rmsnorm(x, w):
    return x * lax.rsqrt(jnp.mean(x * x, axis=-1, keepdims=True) + EPS) * w


_PROJ_SEGS = (
    (0, SSD_WIDTH),
    (SSD_WIDTH, SSD_WIDTH + CONV_DIM),
    (SSD_WIDTH + CONV_DIM, SSD_WIDTH + CONV_DIM + 3 * SB_WIDTH),
    (SSD_WIDTH + CONV_DIM + 3 * SB_WIDTH, D_IN_PERM - DT_PAD),
    (D_IN_PERM - DT_PAD, D_IN_PERM),
)


def _norm_inproj_kernel(x_ref, nw_ref, w_ref, *out_refs):
    hb = _rmsnorm(x_ref[...], nw_ref[...]).astype(BF16)
    for ref, (a, b) in zip(out_refs, _PROJ_SEGS):
        ref[...] = _dot(hb, w_ref[:, a:b])


def _norm_inproj(x2, norm_w, w_perm):
    nt = x2.shape[0]
    grid = (nt // TM_PROJ,)
    row_spec = lambda width: pl.BlockSpec((TM_PROJ, width), lambda i: (i, 0))
    const = lambda shape: pl.BlockSpec(shape, lambda i: (0, 0), pipeline_mode=pl.Buffered(1))
    widths = [b - a for a, b in _PROJ_SEGS]
    return pl.pallas_call(
        _norm_inproj_kernel,
        grid=grid,
        in_specs=[row_spec(D_MODEL), const((1, D_MODEL)), const((D_MODEL, D_IN_PERM))],
        out_specs=[row_spec(w) for w in widths],
        out_shape=[jax.ShapeDtypeStruct((nt, w), F32) for w in widths],
        compiler_params=pltpu.CompilerParams(
            dimension_semantics=("arbitrary",), vmem_limit_bytes=VMEM_LIMIT),
        name="norm_inproj",
    )(x2, norm_w, w_perm)


def _ssd_kernel(z_ref, xbc_ref, dt_ref, convw_ref, convb_ref, dtb_ref, alog_ref, dskip_ref,
                nw_ref, tri_ref, upper_ref, expand_ref, o_ref, ext_ref, state_ref):
    L = CHUNK
    c = pl.program_id(1)

    @pl.when(c == 0)
    def _():
        ext_ref[0:SUBLANES, :] = jnp.zeros((SUBLANES, CONV_DIM), F32)
        state_ref[...] = jnp.zeros_like(state_ref)

    ext_ref[SUBLANES:SUBLANES + L, :] = xbc_ref[...]
    u = convb_ref[...]
    for i in range(CONV_WIDTH):
        off = SUBLANES - (CONV_WIDTH - 1) + i
        u = u + convw_ref[i:i + 1, :] * ext_ref[off:off + L, :]
    ext_ref[0:SUBLANES, :] = ext_ref[L:L + SUBLANES, :]
    act = _silu(u)
    xs = act[:, 0:SSD_WIDTH]
    xs_b = xs.astype(BF16)

    dt = _softplus(dt_ref[...] + dtb_ref[...])
    a = -jnp.exp(alog_ref[...])
    d_a = dt * a

    row = lax.broadcasted_iota(jnp.int32, (L, L), 0)
    col = lax.broadcasted_iota(jnp.int32, (L, L), 1)
    causal = col <= row
    acum = _dot_f32_rhs(tri_ref[...], d_a, 3)
    d_a_t = jnp.transpose(d_a)[0:2 * SUBLANES, :]
    acum_t = _dot_f32_lhs(d_a_t, upper_ref[...], 3)
    dt_t = jnp.transpose(dt)[0:2 * SUBLANES, :]

    total = acum[L - 1:L, :]
    eac = jnp.exp(acum)
    f_state = dt * jnp.exp(total - acum)

    expand = expand_ref[...]
    eac_x = _dot_f32_lhs(eac, expand, 2)
    f_state_x = _dot_f32_lhs(f_state, expand, 2)

    lane = lax.broadcasted_iota(jnp.int32, (L, LANES), 1)
    low_half = lane < SSD_HEAD_DIM

    y_parts = []
    for g in range(SSD_GROUPS):
        b_g = act[:, SSD_WIDTH + g * D_STATE:SSD_WIDTH + (g + 1) * D_STATE].astype(BF16)
        c_off = SSD_WIDTH + SSD_GROUPS * D_STATE
        c_g = act[:, c_off + g * D_STATE:c_off + (g + 1) * D_STATE].astype(BF16)
        cb = lax.dot_general(c_g, b_g, (((1,), (1,)), ((), ())), preferred_element_type=F32)
        gw = SSD_WIDTH // SSD_GROUPS
        st = state_ref[g]
        y_off = _dot(c_g, st.astype(BF16)) * eac_x[:, g * gw:(g + 1) * gw]
        for j in range(2):
            lo = g * gw + j * LANES
            x_pair = xs_b[:, lo:lo + LANES]
            y_pair = None
            for k in range(2):
                h = (lo // SSD_HEAD_DIM) + k
                seg = acum[:, h:h + 1] - acum_t[h:h + 1, :]
                decay = jnp.exp(jnp.where(causal, seg, -jnp.inf))
                m = (cb * decay * dt_t[h:h + 1, :]).astype(BF16)
                x_h = jnp.where(low_half if k == 0 else jnp.logical_not(low_half), x_pair, 0)
                contrib = _dot(m, x_h)
                y_pair = contrib if y_pair is None else y_pair + contrib
            y_parts.append(y_pair + y_off[:, j * LANES:(j + 1) * LANES])
        xd = (xs[:, g * gw:(g + 1) * gw] * f_state_x[:, g * gw:(g + 1) * gw]).astype(BF16)
        s_add = lax.dot_general(b_g, xd, (((0,), (0,)), ((), ())), preferred_element_type=F32)
        state_ref[g] = st * eac_x[L - 1:L, g * gw:(g + 1) * gw] + s_add

    y = jnp.concatenate(y_parts, axis=1) + xs * dskip_ref[...]
    y = y * _silu(z_ref[...])
    o_ref[...] = _rmsnorm(y, nw_ref[...])


def _ssd(z, xbc, dt, conv_w, conv_b, dt_bias, a_log, d_skip_x, norm_w, batch, seq):
    nt = z.shape[0]
    nc = seq // CHUNK
    row = lambda width: pl.BlockSpec((CHUNK, width), lambda b, c: (b * nc + c, 0))
    const = lambda shape: pl.BlockSpec(shape, lambda b, c: (0, 0))
    idx = jnp.arange(CHUNK)
    tri = (idx[None, :] <= idx[:, None]).astype(BF16)
    upper = (idx[:, None] <= idx[None, :]).astype(BF16)
    expand = (jnp.arange(SSD_WIDTH)[None, :] // SSD_HEAD_DIM
              == jnp.arange(DT_PAD)[:, None]).astype(BF16)
    return pl.pallas_call(
        _ssd_kernel,
        grid=(batch, nc),
        in_specs=[row(SSD_WIDTH), row(CONV_DIM), row(DT_PAD),
                  const((CONV_WIDTH, CONV_DIM)), const((1, CONV_DIM)), const((1, DT_PAD)),
                  const((1, DT_PAD)), const((1, SSD_WIDTH)), const((1, SSD_WIDTH)),
                  const((CHUNK, CHUNK)), const((CHUNK, CHUNK)), const((DT_PAD, SSD_WIDTH))],
        out_specs=row(SSD_WIDTH),
        out_shape=jax.ShapeDtypeStruct((nt, SSD_WIDTH), F32),
        scratch_shapes=[pltpu.VMEM((CHUNK + SUBLANES, CONV_DIM), F32),
                        pltpu.VMEM((SSD_GROUPS, D_STATE, SSD_WIDTH // SSD_GROUPS), F32)],
        compiler_params=pltpu.CompilerParams(
            dimension_semantics=("arbitrary", "arbitrary"), vmem_limit_bytes=VMEM_LIMIT),
        name="ssd_mixer",
    )(z, xbc, dt, conv_w, conv_b, dt_bias, a_log, d_skip_x, norm_w, tri, upper, expand)


def _sb_kernel(q_ref, k_ref, v_ref, o_ref, kb_ref, vbm_ref, q4_ref, acc_ref, carry_ref):
    T = SB_BLOCK
    H = SB_HEADS
    qi = pl.program_id(1)
    head_of_lane = jnp.right_shift(lax.broadcasted_iota(jnp.int32, (1, SB_WIDTH), 1),
                                   HEAD_DIM_SHIFT)

    @pl.when(qi == 0)
    def _():
        kb_ref[...] = k_ref[...].astype(BF16)
        v = v_ref[...]
        for h in range(H):
            vbm_ref[h] = jnp.where(head_of_lane == h, v, 0.0).astype(BF16)

    q = q_ref[...] * (SB_HEAD_DIM ** -0.5)
    for h in range(H):
        q4_ref[h * T:(h + 1) * T, :] = jnp.where(head_of_lane == h, q, 0.0).astype(BF16)

    row = lax.broadcasted_iota(jnp.int32, (T, T), 0)
    col = lax.broadcasted_iota(jnp.int32, (T, T), 1)
    before = col < row
    row2 = lax.broadcasted_iota(jnp.int32, (2 * T, T), 0) & (T - 1)
    tri2 = (row2 > lax.broadcasted_iota(jnp.int32, (2 * T, T), 1)).astype(BF16)

    def block(j, carries, diagonal):
        start = pl.multiple_of(j * T, T)
        kb = kb_ref[pl.ds(start, T), :]
        z4 = lax.dot_general(q4_ref[...], kb, (((1,), (1,)), ((), ())),
                             preferred_element_type=F32)
        ws = []
        new_carries = []
        for h in range(H):
            z = z4[h * T:(h + 1) * T, :]
            e = jnp.exp2(jnp.abs(z) * (-LOG2E))
            sp = jnp.maximum(z, 0.0) + jnp.log(1.0 + e)
            if diagonal:
                sp = jnp.where(before, sp, 0.0)
            hi = lax.bitcast_convert_type(
                lax.bitcast_convert_type(sp, jnp.uint32) & jnp.uint32(0xFFFF0000), F32)
            hl = jnp.concatenate([hi.astype(BF16), (sp - hi).astype(BF16)], axis=1)
            arg = z - sp - _dot(hl, tri2)
            if carries is not None:
                arg = arg - carries[h]
            w = jnp.exp2(arg * LOG2E)
            if diagonal:
                w = jnp.where(before, w, 0.0)
            ws.append(w.astype(BF16))
            total = jnp.sum(sp, axis=1, keepdims=True)
            new_carries.append(total if carries is None else carries[h] + total)
        pv = _dot(ws[0], vbm_ref[0, pl.ds(start, T), :])
        for h in range(1, H):
            pv = pv + _dot(ws[h], vbm_ref[h, pl.ds(start, T), :])
        return pv, tuple(new_carries)

    pv, carries = block(qi, None, True)
    acc_ref[...] = pv
    for h in range(H):
        carry_ref[h] = carries[h]

    @pl.when((qi & 1) == 1)
    def _():
        pv, carries = block(qi - 1, tuple(carry_ref[h] for h in range(H)), False)
        acc_ref[...] += pv
        for h in range(H):
            carry_ref[h] = carries[h]

    def body(it, carries):
        j = (qi & ~1) - 1 - 2 * it
        pv0, carries = block(j, carries, False)
        pv1, carries = block(j - 1, carries, False)
        acc_ref[...] += pv0 + pv1
        return carries

    lax.fori_loop(0, qi >> 1, body, tuple(carry_ref[h] for h in range(H)))
    o_ref[...] = acc_ref[...]


def _sb_attention(qkv, batch, seq):
    nt = qkv.shape[0]
    nq = seq // SB_BLOCK
    return pl.pallas_call(
        _sb_kernel,
        grid=(batch, nq),
        in_specs=[pl.BlockSpec((SB_BLOCK, SB_WIDTH), lambda b, i: (b * nq + i, 0)),
                  pl.BlockSpec((seq, SB_WIDTH), lambda b, i: (b, 1)),
                  pl.BlockSpec((seq, SB_WIDTH), lambda b, i: (b, 2))],
        out_specs=pl.BlockSpec((SB_BLOCK, SB_WIDTH), lambda b, i: (b * nq + i, 0)),
        out_shape=jax.ShapeDtypeStruct((nt, SB_WIDTH), F32),
        scratch_shapes=[pltpu.VMEM((seq, SB_WIDTH), BF16),
                        pltpu.VMEM((SB_HEADS, seq, SB_WIDTH), BF16),
                        pltpu.VMEM((SB_HEADS * SB_BLOCK, SB_WIDTH), BF16),
                        pltpu.VMEM((SB_BLOCK, SB_WIDTH), F32),
                        pltpu.VMEM((SB_HEADS, SB_BLOCK, 1), F32)],
        compiler_params=pltpu.CompilerParams(
            dimension_semantics=("arbitrary", "arbitrary"), vmem_limit_bytes=VMEM_LIMIT),
        name="sb_attention",
    )(qkv, qkv, qkv)


def _pool_out_ffn_kernel(x_ref, yssd_ref, ysb_ref, p_ref, halo_ref, poolw_ref, poolb_ref,
                         pscale_ref, wout_ref, n2_ref, wg_ref, wu_ref, wd_ref, fn_ref,
                         o_ref, buf_a, buf_b, *, tiles_per_seq, final_norm):
    TM = TM_FFN
    i = pl.program_id(0)
    t0 = (i % tiles_per_seq) * TM

    top = SUBLANES + POOL_HALO
    p_cur = p_ref[...]
    zeros8 = jnp.zeros((SUBLANES, POOL_WIDTH), F32)
    buf_a[0:SUBLANES, :] = zeros8
    buf_b[0:SUBLANES, :] = zeros8
    buf_a[SUBLANES:top, :] = jnp.where(t0 == 0, 0.0, halo_ref[...])
    buf_a[top:top + TM, :] = p_cur
    lane = lax.broadcasted_iota(jnp.int32, (TM, POOL_WIDTH), 1)
    group = jnp.right_shift(lane, HEAD_DIM_SHIFT)
    n_rows = TM + POOL_HALO
    wsum = None
    src, dst = buf_a, buf_b
    for gi, win in enumerate(POOL_WINDOWS):
        shift = win // 2
        s = src[SUBLANES:SUBLANES + n_rows, :] + src[SUBLANES - shift:SUBLANES - shift + n_rows, :]
        dst[SUBLANES:SUBLANES + n_rows, :] = s
        tile_sum = s[POOL_HALO:, :]
        wsum = tile_sum if wsum is None else jnp.where(group >= gi, tile_sum, wsum)
        src, dst = dst, src
    pos1 = t0 + 1 + lax.broadcasted_iota(jnp.int32, (TM, POOL_WIDTH), 0)
    win_lane = jnp.left_shift(2, group)
    count = jnp.minimum(pos1, win_lane).astype(F32)
    pooled = wsum / count - p_cur
    mixed = _dot(pooled.astype(BF16), poolw_ref[...]) + poolb_ref[...]
    y_pool = mixed * pscale_ref[...]

    a0, a1 = SSD_WIDTH, SSD_WIDTH + SB_WIDTH
    o_ref[...] = (x_ref[...]
                  + _dot(yssd_ref[...].astype(BF16), wout_ref[0:a0, :])
                  + _dot(ysb_ref[...].astype(BF16), wout_ref[a0:a1, :])
                  + _dot(y_pool.astype(BF16), wout_ref[a1:, :]))

    x1 = o_ref[...]
    hb = _rmsnorm(x1, n2_ref[...]).astype(BF16)
    acc = x1
    for lo, hi in ((0, FF_SPLIT), (FF_SPLIT, D_FF)):
        gate = _dot(hb, wg_ref[:, lo:hi])
        up = _dot(hb, wu_ref[:, lo:hi])
        act = (_silu(gate) * up).astype(BF16)
        acc = acc + _dot(act, wd_ref[lo:hi, :])
    if final_norm:
        acc = _rmsnorm(acc, fn_ref[...])
    o_ref[...] = acc


def _pool_out_ffn(x2, y_ssd, y_sb, p, pool_w_bd, pool_b, pool_scale, w_out, norm2_w, w_gate,
                  w_up, w_down, final_w, seq, final_norm):
    nt = x2.shape[0]
    tiles_per_seq = seq // TM_FFN
    halo_per_tile = TM_FFN // POOL_HALO
    row = lambda width: pl.BlockSpec((TM_FFN, width), lambda i: (i, 0))
    const = lambda shape: pl.BlockSpec(shape, lambda i: (0, 0), pipeline_mode=pl.Buffered(1))
    halo = pl.BlockSpec((POOL_HALO, POOL_WIDTH),
                        lambda i: (jnp.maximum(i * halo_per_tile - 1, 0), 0))
    kern = functools.partial(_pool_out_ffn_kernel, tiles_per_seq=tiles_per_seq,
                             final_norm=final_norm)
    buf_rows = SUBLANES + POOL_HALO + TM_FFN
    return pl.pallas_call(
        kern,
        grid=(nt // TM_FFN,),
        in_specs=[row(D_MODEL), row(SSD_WIDTH), row(SB_WIDTH), row(POOL_WIDTH), halo,
                  const((POOL_WIDTH, POOL_WIDTH)), const((1, POOL_WIDTH)), const((1, POOL_WIDTH)),
                  const((D_MODEL, D_MODEL)), const((1, D_MODEL)),
                  const((D_MODEL, D_FF)), const((D_MODEL, D_FF)), const((D_FF, D_MODEL)),
                  const((1, D_MODEL))],
        out_specs=row(D_MODEL),
        out_shape=jax.ShapeDtypeStruct((nt, D_MODEL), F32),
        scratch_shapes=[pltpu.VMEM((buf_rows, POOL_WIDTH), F32),
                        pltpu.VMEM((buf_rows, POOL_WIDTH), F32)],
        compiler_params=pltpu.CompilerParams(
            dimension_semantics=("arbitrary",), vmem_limit_bytes=VMEM_LIMIT),
        name="pool_out_ffn",
    )(x2, y_ssd, y_sb, p, p, pool_w_bd, pool_b, pool_scale, w_out, norm2_w, w_gate, w_up,
      w_down, final_w)


def _permute_w_in(w_in):
    s0 = SSD_WIDTH + CONV_DIM
    s1 = s0 + SSD_HEADS
    pad = jnp.zeros((w_in.shape[0], DT_PAD - SSD_HEADS), w_in.dtype)
    return jnp.concatenate([w_in[:, :s0], w_in[:, s1:], w_in[:, s0:s1], pad], axis=1)


def _block_diag(pool_w):
    g, c, d = pool_w.shape
    out = jnp.zeros((g * c, g * d), pool_w.dtype)
    for i in range(g):
        out = out.at[i * c:(i + 1) * c, i * d:(i + 1) * d].set(pool_w[i])
    return out


def _pad_lanes(v, width):
    return jnp.pad(v, (0, width - v.shape[0])).reshape(1, width)


def kernel(x, norm1_w, w_in, conv_w, conv_b, dt_bias, a_log, d_skip, ssd_norm_w, pool_w, pool_b,
           pool_scale, w_out, norm2_w, w_gate, w_up, w_down, final_norm_w):
    batch, seq, d_model = x.shape
    x2 = x.reshape(batch * seq, d_model)
    final_w = final_norm_w.reshape(1, d_model)
    w_out, w_gate, w_up, w_down = (w.astype(BF16) for w in (w_out, w_gate, w_up, w_down))
    w_in = w_in.astype(BF16)
    for layer in range(DEPTH):
        w_perm = _permute_w_in(w_in[layer])
        z, xbc, qkv, p, dt = _norm_inproj(x2, norm1_w[layer].reshape(1, d_model), w_perm)
        y_ssd = _ssd(z, xbc, dt, conv_w[layer], conv_b[layer].reshape(1, CONV_DIM),
                     _pad_lanes(dt_bias[layer], DT_PAD), _pad_lanes(a_log[layer], DT_PAD),
                     jnp.repeat(d_skip[layer], SSD_HEAD_DIM).reshape(1, SSD_WIDTH),
                     ssd_norm_w[layer].reshape(1, SSD_WIDTH), batch, seq)
        y_sb = _sb_attention(qkv, batch, seq)
        x2 = _pool_out_ffn(
            x2, y_ssd, y_sb, p, _block_diag(pool_w[layer]).astype(BF16),
            pool_b[layer].reshape(1, POOL_WIDTH), pool_scale[layer].reshape(1, POOL_WIDTH),
            w_out[layer], norm2_w[layer].reshape(1, d_model),
            w_gate[layer], w_up[layer], w_down[layer],
            final_w, seq, final_norm=(layer == DEPTH - 1))
    return x2.reshape(batch, seq, d_model)
```

```python
import functools

import jax
import jax.numpy as jnp
from jax import lax
from jax.experimental import pallas as pl
from jax.experimental.pallas import tpu as pltpu

F32 = jnp.float32
BF16 = jnp.bfloat16

D_MODEL = 1024
DEPTH = 4
SSD_WIDTH = 512
SSD_HEADS = 8
SSD_HEAD_DIM = 64
SSD_GROUPS = 2
D_STATE = 128
CONV_WIDTH = 4
CHUNK = 128
CONV_DIM = SSD_WIDTH + 2 * SSD_GROUPS * D_STATE
SB_WIDTH = 256
SB_HEADS = 4
SB_HEAD_DIM = 64
POOL_WINDOWS = (2, 4, 8, 16)
POOL_WIDTH = 256
POOL_GROUP_DIM = 64
D_FF = 2816
EPS = 1e-6
HEAD_DIM_SHIFT = 6
LOG2E = 1.4426950408889634

LANES = 128
SUBLANES = 8
DT_PAD = LANES
D_IN_PERM = SSD_WIDTH + CONV_DIM + 3 * SB_WIDTH + POOL_WIDTH + DT_PAD
VMEM_LIMIT = 56 * 1024 * 1024

TM_PROJ = 512
TM_FFN = 512
FF_SPLIT = 1536
SB_BLOCK = 256
POOL_HALO = 16


def _dot(a, b):
    return jnp.dot(a, b, preferred_element_type=F32)


def _softplus(x):
    return jnp.maximum(x, 0.0) + jnp.log1p(jnp.exp(-jnp.abs(x)))


def _silu(x):
    return x / (1.0 + jnp.exp(-x))


def _split_bf16(x, n):
    parts = []
    r = x
    for i in range(n):
        p = r.astype(BF16)
        parts.append(p)
        if i + 1 < n:
            r = r - p.astype(F32)
    return parts


def _dot_f32_lhs(x, rhs_exact, n):
    parts = _split_bf16(x, n)
    acc = _dot(parts[0], rhs_exact)
    for p in parts[1:]:
        acc = acc + _dot(p, rhs_exact)
    return acc


def _dot_f32_rhs(lhs_exact, x, n):
    parts = _split_bf16(x, n)
    acc = _dot(lhs_exact, parts[0])
    for p in parts[1:]:
        acc = acc + _dot(lhs_exact, p)
    return acc


def _rmsnorm(x, w):
    return x * lax.rsqrt(jnp.mean(x * x, axis=-1, keepdims=True) + EPS) * w


_Z_COLS = (0, SSD_WIDTH)
_XBC_COLS = (SSD_WIDTH, SSD_WIDTH + CONV_DIM)
_QKV_COLS = (_XBC_COLS[1], _XBC_COLS[1] + 3 * SB_WIDTH)
_P_COLS = (_QKV_COLS[1], D_IN_PERM - DT_PAD)
_DT_COLS = (D_IN_PERM - DT_PAD, D_IN_PERM)


def _layer_spec(shape, layer, grid_rank, single_buffer=False):
    zeros = (0,) * len(shape)
    index_map = {1: lambda i: (layer,) + zeros, 2: lambda i, j: (layer,) + zeros}[grid_rank]
    kwargs = {"pipeline_mode": pl.Buffered(1)} if single_buffer else {}
    return pl.BlockSpec((None,) + tuple(shape), index_map, **kwargs)


_PROJ_SEGS = (_Z_COLS, _XBC_COLS, _QKV_COLS, _P_COLS, _DT_COLS)


def _norm_inproj_kernel(x_ref, nw_ref, w_ref, *out_refs):
    hb = _rmsnorm(x_ref[...], nw_ref[...]).astype(BF16)
    for ref, (lo, hi) in zip(out_refs, _PROJ_SEGS):
        ref[...] = _dot(hb, w_ref[:, lo:hi])


def _norm_inproj(x2, norm_w, w_perm, layer):
    nt = x2.shape[0]
    row = lambda width: pl.BlockSpec((TM_PROJ, width), lambda i: (i, 0))
    widths = [hi - lo for lo, hi in _PROJ_SEGS]
    return pl.pallas_call(
        _norm_inproj_kernel,
        grid=(nt // TM_PROJ,),
        in_specs=[row(D_MODEL), _layer_spec((1, D_MODEL), layer, 1),
                  _layer_spec((D_MODEL, D_IN_PERM), layer, 1, single_buffer=True)],
        out_specs=[row(w) for w in widths],
        out_shape=[jax.ShapeDtypeStruct((nt, w), F32) for w in widths],
        compiler_params=pltpu.CompilerParams(
            dimension_semantics=("arbitrary",), vmem_limit_bytes=VMEM_LIMIT),
        name="norm_inproj",
    )(x2, norm_w, w_perm)


def _ssd_kernel(z_ref, xbc_ref, dt_ref, convw_ref, convb_ref, dtb_ref, alog_ref, dskip_ref,
                nw_ref, tri_ref, upper_ref, expand_ref, o_ref, ext_ref, state_ref):
    L = CHUNK
    c = pl.program_id(1)

    @pl.when(c == 0)
    def _():
        ext_ref[0:SUBLANES, :] = jnp.zeros((SUBLANES, CONV_DIM), F32)
        state_ref[...] = jnp.zeros_like(state_ref)

    ext_ref[SUBLANES:SUBLANES + L, :] = xbc_ref[...]
    u = convb_ref[...]
    for i in range(CONV_WIDTH):
        off = SUBLANES - (CONV_WIDTH - 1) + i
        u = u + convw_ref[i:i + 1, :] * ext_ref[off:off + L, :]
    ext_ref[0:SUBLANES, :] = ext_ref[L:L + SUBLANES, :]
    act = _silu(u)
    xs = act[:, 0:SSD_WIDTH]
    xs_b = xs.astype(BF16)
    bc = act[:, SSD_WIDTH:CONV_DIM].astype(BF16)

    dt = _softplus(dt_ref[...] + dtb_ref[...])
    a = -jnp.exp(alog_ref[...])
    d_a = dt * a

    row = lax.broadcasted_iota(jnp.int32, (L, L), 0)
    col = lax.broadcasted_iota(jnp.int32, (L, L), 1)
    causal = col <= row
    acum = _dot_f32_rhs(tri_ref[...], d_a, 3)
    d_a_t = jnp.transpose(d_a)[0:2 * SUBLANES, :]
    acum_t = _dot_f32_lhs(d_a_t, upper_ref[...], 3)
    dt_t = jnp.transpose(dt)[0:2 * SUBLANES, :]

    total = acum[L - 1:L, :]
    eac = jnp.exp(acum)
    f_state = dt * jnp.exp(total - acum)

    expand = expand_ref[...]
    eac_x = _dot_f32_lhs(eac, expand, 2)
    f_state_x = _dot_f32_lhs(f_state, expand, 2)

    lane = lax.broadcasted_iota(jnp.int32, (L, LANES), 1)
    low_half = lane < SSD_HEAD_DIM

    y_parts = []
    for g in range(SSD_GROUPS):
        b_g = bc[:, g * D_STATE:(g + 1) * D_STATE]
        c_off = SSD_GROUPS * D_STATE
        c_g = bc[:, c_off + g * D_STATE:c_off + (g + 1) * D_STATE]
        cb = lax.dot_general(c_g, b_g, (((1,), (1,)), ((), ())), preferred_element_type=F32)
        gw = SSD_WIDTH // SSD_GROUPS
        st = state_ref[g]
        y_off = _dot(c_g, st.astype(BF16)) * eac_x[:, g * gw:(g + 1) * gw]
        for j in range(2):
            lo = g * gw + j * LANES
            x_pair = xs_b[:, lo:lo + LANES]
            y_pair = None
            for k in range(2):
                h = (lo // SSD_HEAD_DIM) + k
                seg = acum[:, h:h + 1] - acum_t[h:h + 1, :]
                decay = jnp.exp(jnp.where(causal, seg, -jnp.inf))
                m = (cb * decay * dt_t[h:h + 1, :]).astype(BF16)
                x_h = jnp.where(low_half if k == 0 else jnp.logical_not(low_half), x_pair, 0)
                contrib = _dot(m, x_h)
                y_pair = contrib if y_pair is None else y_pair + contrib
            y_parts.append(y_pair + y_off[:, j * LANES:(j + 1) * LANES])
        xd = (xs[:, g * gw:(g + 1) * gw] * f_state_x[:, g * gw:(g + 1) * gw]).astype(BF16)
        s_add = lax.dot_general(b_g, xd, (((0,), (0,)), ((), ())), preferred_element_type=F32)
        state_ref[g] = st * eac_x[L - 1:L, g * gw:(g + 1) * gw] + s_add

    y = jnp.concatenate(y_parts, axis=1) + xs * dskip_ref[...]
    y = y * _silu(z_ref[...])
    o_ref[...] = _rmsnorm(y, nw_ref[...])


def _ssd(z, xbc, dt, conv_w, conv_b, dt_bias, a_log, d_skip_x, norm_w, consts, layer, batch,
         seq):
    nt = z.shape[0]
    nc = seq // CHUNK
    row = lambda width: pl.BlockSpec((CHUNK, width), lambda b, c: (b * nc + c, 0))
    const = lambda shape: pl.BlockSpec(shape, lambda b, c: (0, 0))
    spec = lambda shape: _layer_spec(shape, layer, 2)
    tri, upper, expand = consts
    return pl.pallas_call(
        _ssd_kernel,
        grid=(batch, nc),
        in_specs=[row(SSD_WIDTH), row(CONV_DIM), row(DT_PAD),
                  spec((CONV_WIDTH, CONV_DIM)), spec((1, CONV_DIM)), spec((1, DT_PAD)),
                  spec((1, DT_PAD)), spec((1, SSD_WIDTH)), spec((1, SSD_WIDTH)),
                  const((CHUNK, CHUNK)), const((CHUNK, CHUNK)), const((DT_PAD, SSD_WIDTH))],
        out_specs=row(SSD_WIDTH),
        out_shape=jax.ShapeDtypeStruct((nt, SSD_WIDTH), F32),
        scratch_shapes=[pltpu.VMEM((CHUNK + SUBLANES, CONV_DIM), F32),
                        pltpu.VMEM((SSD_GROUPS, D_STATE, SSD_WIDTH // SSD_GROUPS), F32)],
        compiler_params=pltpu.CompilerParams(
            dimension_semantics=("arbitrary", "arbitrary"), vmem_limit_bytes=VMEM_LIMIT),
        name="ssd_mixer",
    )(z, xbc, dt, conv_w, conv_b, dt_bias, a_log, d_skip_x, norm_w, tri, upper, expand)


def _ssd_constants():
    idx = jnp.arange(CHUNK)
    tri = (idx[None, :] <= idx[:, None]).astype(BF16)
    upper = (idx[:, None] <= idx[None, :]).astype(BF16)
    expand = (jnp.arange(SSD_WIDTH)[None, :] // SSD_HEAD_DIM
              == jnp.arange(DT_PAD)[:, None]).astype(BF16)
    return tri, upper, expand


def _sb_kernel(q_ref, k_ref, v_ref, o_ref, kb_ref, vbm_ref, q4_ref, acc_ref, carry_ref):
    T = SB_BLOCK
    H = SB_HEADS
    qi = pl.program_id(1)
    head_of_lane = jnp.right_shift(lax.broadcasted_iota(jnp.int32, (1, SB_WIDTH), 1),
                                   HEAD_DIM_SHIFT)

    @pl.when(qi == 0)
    def _():
        kb_ref[...] = k_ref[...].astype(BF16)
        v = v_ref[...]
        for h in range(H):
            vbm_ref[h] = jnp.where(head_of_lane == h, v, 0.0).astype(BF16)

    q = q_ref[...] * (SB_HEAD_DIM ** -0.5)
    for h in range(H):
        q4_ref[h * T:(h + 1) * T, :] = jnp.where(head_of_lane == h, q, 0.0).astype(BF16)

    row = lax.broadcasted_iota(jnp.int32, (T, T), 0)
    col = lax.broadcasted_iota(jnp.int32, (T, T), 1)
    before = col < row
    row2 = lax.broadcasted_iota(jnp.int32, (2 * T, T), 0) & (T - 1)
    tri2 = (row2 > lax.broadcasted_iota(jnp.int32, (2 * T, T), 1)).astype(BF16)

    def block(j, carries, diagonal):
        start = pl.multiple_of(j * T, T)
        kb = kb_ref[pl.ds(start, T), :]
        z4 = lax.dot_general(q4_ref[...], kb, (((1,), (1,)), ((), ())),
                             preferred_element_type=F32)
        ws = []
        new_carries = []
        for h in range(H):
            z = z4[h * T:(h + 1) * T, :]
            e = jnp.exp2(jnp.abs(z) * (-LOG2E))
            sp = jnp.maximum(z, 0.0) + jnp.log(1.0 + e)
            if diagonal:
                sp = jnp.where(before, sp, 0.0)
            hi = sp.astype(BF16)
            hl = jnp.concatenate([hi, (sp - hi.astype(F32)).astype(BF16)], axis=1)
            arg = z - sp - _dot(hl, tri2)
            if carries is not None:
                arg = arg - jnp.sum(carries[h], axis=1, keepdims=True)
            w = jnp.exp2(arg * LOG2E)
            if diagonal:
                w = jnp.where(before, w, 0.0)
            ws.append(w.astype(BF16))
            part = sp[:, 0:LANES] + sp[:, LANES:2 * LANES]
            new_carries.append(part if carries is None else carries[h] + part)
        pv = _dot(ws[0], vbm_ref[0, pl.ds(start, T), :])
        for h in range(1, H):
            pv = pv + _dot(ws[h], vbm_ref[h, pl.ds(start, T), :])
        return pv, tuple(new_carries)

    def save(pv, carries, first):
        if first:
            acc_ref[...] = pv
        else:
            acc_ref[...] += pv
        for h in range(H):
            carry_ref[h] = carries[h]

    def saved_carries():
        return tuple(carry_ref[h] for h in range(H))

    @pl.when(qi == 0)
    def _():
        pv, carries = block(qi, None, True)
        save(pv, carries, True)

    @pl.when(qi > 0)
    def _():
        pv0, carries = block(qi, None, True)
        pv1, carries = block(qi - 1, carries, False)
        save(pv0 + pv1, carries, True)

    rest = jnp.maximum(qi - 1, 0)
    odd = rest & 1

    @pl.when(odd == 1)
    def _():
        pv, carries = block(qi - 2, saved_carries(), False)
        save(pv, carries, False)

    def body(it, carries):
        j = qi - 2 - odd - 2 * it
        pv0, carries = block(j, carries, False)
        pv1, carries = block(j - 1, carries, False)
        acc_ref[...] += pv0 + pv1
        return carries

    lax.fori_loop(0, rest >> 1, body, saved_carries())
    o_ref[...] = acc_ref[...]


def _sb_attention(qkv, batch, seq):
    nt = qkv.shape[0]
    nq = seq // SB_BLOCK
    return pl.pallas_call(
        _sb_kernel,
        grid=(batch, nq),
        in_specs=[pl.BlockSpec((SB_BLOCK, SB_WIDTH), lambda b, i: (b * nq + i, 0)),
                  pl.BlockSpec((seq, SB_WIDTH), lambda b, i: (b, 1)),
                  pl.BlockSpec((seq, SB_WIDTH), lambda b, i: (b, 2))],
        out_specs=pl.BlockSpec((SB_BLOCK, SB_WIDTH), lambda b, i: (b * nq + i, 0)),
        out_shape=jax.ShapeDtypeStruct((nt, SB_WIDTH), F32),
        scratch_shapes=[pltpu.VMEM((seq, SB_WIDTH), BF16),
                        pltpu.VMEM((SB_HEADS, seq, SB_WIDTH), BF16),
                        pltpu.VMEM((SB_HEADS * SB_BLOCK, SB_WIDTH), BF16),
                        pltpu.VMEM((SB_BLOCK, SB_WIDTH), F32),
                        pltpu.VMEM((SB_HEADS, SB_BLOCK, LANES), F32)],
        compiler_params=pltpu.CompilerParams(
            dimension_semantics=("arbitrary", "arbitrary"), vmem_limit_bytes=VMEM_LIMIT),
        name="sb_attention",
    )(qkv, qkv, qkv)


def _pool_out_ffn_kernel(x_ref, yssd_ref, ysb_ref, p_ref, halo_ref, poolw_ref, poolb_ref,
                         pscale_ref, wout_ref, n2_ref, wg_ref, wu_ref, wd_ref, fn_ref,
                         o_ref, buf_a, buf_b, *, tiles_per_seq, final_norm):
    TM = TM_FFN
    i = pl.program_id(0)
    t0 = (i % tiles_per_seq) * TM

    top = SUBLANES + POOL_HALO
    p_cur = p_ref[...]
    zeros8 = jnp.zeros((SUBLANES, POOL_WIDTH), F32)
    buf_a[0:SUBLANES, :] = zeros8
    buf_b[0:SUBLANES, :] = zeros8
    buf_a[SUBLANES:top, :] = jnp.where(t0 == 0, 0.0, halo_ref[...])
    buf_a[top:top + TM, :] = p_cur
    lane = lax.broadcasted_iota(jnp.int32, (TM, POOL_WIDTH), 1)
    group = jnp.right_shift(lane, HEAD_DIM_SHIFT)
    n_rows = TM + POOL_HALO
    wsum = None
    src, dst = buf_a, buf_b
    for gi, win in enumerate(POOL_WINDOWS):
        shift = win // 2
        s = src[SUBLANES:SUBLANES + n_rows, :] + src[SUBLANES - shift:SUBLANES - shift + n_rows, :]
        dst[SUBLANES:SUBLANES + n_rows, :] = s
        tile_sum = s[POOL_HALO:, :]
        wsum = tile_sum if wsum is None else jnp.where(group >= gi, tile_sum, wsum)
        src, dst = dst, src
    pos1 = t0 + 1 + lax.broadcasted_iota(jnp.int32, (TM, POOL_WIDTH), 0)
    win_lane = jnp.left_shift(2, group)
    count = jnp.minimum(pos1, win_lane).astype(F32)
    pooled = wsum / count - p_cur
    mixed = _dot(pooled.astype(BF16), poolw_ref[...]) + poolb_ref[...]
    y_pool = mixed * pscale_ref[...]

    a0, a1 = SSD_WIDTH, SSD_WIDTH + SB_WIDTH
    o_ref[...] = (x_ref[...]
                  + _dot(yssd_ref[...].astype(BF16), wout_ref[0:a0, :])
                  + _dot(ysb_ref[...].astype(BF16), wout_ref[a0:a1, :])
                  + _dot(y_pool.astype(BF16), wout_ref[a1:, :]))

    x1 = o_ref[...]
    hb = _rmsnorm(x1, n2_ref[...]).astype(BF16)
    acc = x1
    for lo, hi in ((0, FF_SPLIT), (FF_SPLIT, D_FF)):
        gate = _dot(hb, wg_ref[:, lo:hi])
        up = _dot(hb, wu_ref[:, lo:hi])
        act = (_silu(gate) * up).astype(BF16)
        acc = acc + _dot(act, wd_ref[lo:hi, :])
    if final_norm:
        acc = _rmsnorm(acc, fn_ref[...])
    o_ref[...] = acc


def _pool_out_ffn(x2, y_ssd, y_sb, p, pool_w_bd, pool_b, pool_scale, w_out, norm2_w, w_gate,
                  w_up, w_down, final_w, layer, seq, final_norm):
    nt = x2.shape[0]
    tiles_per_seq = seq // TM_FFN
    halo_per_tile = TM_FFN // POOL_HALO
    row = lambda width: pl.BlockSpec((TM_FFN, width), lambda i: (i, 0))
    spec = lambda shape: _layer_spec(shape, layer, 1, single_buffer=True)
    halo = pl.BlockSpec((POOL_HALO, POOL_WIDTH),
                        lambda i: (jnp.maximum(i * halo_per_tile - 1, 0), 0))
    kern = functools.partial(_pool_out_ffn_kernel, tiles_per_seq=tiles_per_seq,
                             final_norm=final_norm)
    buf_rows = SUBLANES + POOL_HALO + TM_FFN
    return pl.pallas_call(
        kern,
        grid=(nt // TM_FFN,),
        in_specs=[row(D_MODEL), row(SSD_WIDTH), row(SB_WIDTH), row(POOL_WIDTH), halo,
                  spec((POOL_WIDTH, POOL_WIDTH)), spec((1, POOL_WIDTH)), spec((1, POOL_WIDTH)),
                  spec((D_MODEL, D_MODEL)), spec((1, D_MODEL)),
                  spec((D_MODEL, D_FF)), spec((D_MODEL, D_FF)), spec((D_FF, D_MODEL)),
                  pl.BlockSpec((1, D_MODEL), lambda i: (0, 0))],
        out_specs=row(D_MODEL),
        out_shape=jax.ShapeDtypeStruct((nt, D_MODEL), F32),
        scratch_shapes=[pltpu.VMEM((buf_rows, POOL_WIDTH), F32),
                        pltpu.VMEM((buf_rows, POOL_WIDTH), F32)],
        compiler_params=pltpu.CompilerParams(
            dimension_semantics=("arbitrary",), vmem_limit_bytes=VMEM_LIMIT),
        name="pool_out_ffn",
    )(x2, y_ssd, y_sb, p, p, pool_w_bd, pool_b, pool_scale, w_out, norm2_w, w_gate, w_up,
      w_down, final_w)


def _permute_w_in(w_in):
    s0 = SSD_WIDTH + CONV_DIM
    s1 = s0 + SSD_HEADS
    pad = jnp.zeros(w_in.shape[:-1] + (DT_PAD - SSD_HEADS,), w_in.dtype)
    return jnp.concatenate([w_in[..., :s0], w_in[..., s1:], w_in[..., s0:s1], pad], axis=-1)


def _block_diag(pool_w):
    depth, g, c, d = pool_w.shape
    eye = jnp.eye(g, dtype=pool_w.dtype)
    return (pool_w[:, :, :, None, :] * eye[None, :, None, :, None]).reshape(depth, g * c, g * d)


def _rows(v, width=None):
    if width is not None and width > v.shape[-1]:
        v = jnp.pad(v, ((0, 0), (0, width - v.shape[-1])))
    return v[:, None, :]


def kernel(x, norm1_w, w_in, conv_w, conv_b, dt_bias, a_log, d_skip, ssd_norm_w, pool_w, pool_b,
           pool_scale, w_out, norm2_w, w_gate, w_up, w_down, final_norm_w):
    batch, seq, d_model = x.shape
    x2 = x.reshape(batch * seq, d_model)
    w_in_p = _permute_w_in(w_in).astype(BF16)
    w_out, w_gate, w_up, w_down = (w.astype(BF16) for w in (w_out, w_gate, w_up, w_down))
    pool_w_bd = _block_diag(pool_w).astype(BF16)
    norm1_r, norm2_r, conv_b_r = _rows(norm1_w), _rows(norm2_w), _rows(conv_b)
    dt_bias_r, a_log_r = _rows(dt_bias, DT_PAD), _rows(a_log, DT_PAD)
    d_skip_r = _rows(jnp.repeat(d_skip, SSD_HEAD_DIM, axis=1))
    ssd_norm_r = _rows(ssd_norm_w)
    pool_b_r = _rows(pool_b.reshape(DEPTH, POOL_WIDTH))
    pool_scale_r = _rows(pool_scale)
    final_w = final_norm_w.reshape(1, d_model)
    ssd_consts = _ssd_constants()
    for layer in range(DEPTH):
        z, xbc, qkv, p, dt = _norm_inproj(x2, norm1_r, w_in_p, layer)
        y_ssd = _ssd(z, xbc, dt, conv_w, conv_b_r, dt_bias_r, a_log_r, d_skip_r, ssd_norm_r,
                     ssd_consts, layer, batch, seq)
        y_sb = _sb_attention(qkv, batch, seq)
        x2 = _pool_out_ffn(x2, y_ssd, y_sb, p, pool_w_bd, pool_b_r, pool_scale_r, w_out, norm2_r,
                           w_gate, w_up, w_down, final_w, layer, seq,
                           final_norm=(layer == DEPTH - 1))
    return x2.reshape(batch, seq, d_model)
```

```python
import functools

import jax
import jax.numpy as jnp
from jax import lax
from jax.experimental import pallas as pl
from jax.experimental.pallas import tpu as pltpu

F32 = jnp.float32
BF16 = jnp.bfloat16

D_MODEL = 1024
DEPTH = 4
SSD_WIDTH = 512
SSD_HEADS = 8
SSD_HEAD_DIM = 64
SSD_GROUPS = 2
D_STATE = 128
CONV_WIDTH = 4
CHUNK = 128
CONV_DIM = SSD_WIDTH + 2 * SSD_GROUPS * D_STATE
SB_WIDTH = 256
SB_HEADS = 4
SB_HEAD_DIM = 64
POOL_WINDOWS = (2, 4, 8, 16)
POOL_WIDTH = 256
POOL_GROUP_DIM = 64
D_FF = 2816
EPS = 1e-6
HEAD_DIM_SHIFT = 6
LOG2E = 1.4426950408889634

LANES = 128
SUBLANES = 8
DT_PAD = LANES
D_IN_PERM = SSD_WIDTH + CONV_DIM + 3 * SB_WIDTH + POOL_WIDTH + DT_PAD
VMEM_LIMIT = 56 * 1024 * 1024

TM_PROJ = 512
TM_FFN = 512
FF_SPLIT = 1536
SB_BLOCK = 256
SSD_BATCH_TILE = 4
POOL_HALO = 16


def _dot(a, b):
    return jnp.dot(a, b, preferred_element_type=F32)


def _softplus(x):
    return jnp.maximum(x, 0.0) + jnp.log1p(jnp.exp(-jnp.abs(x)))


def _silu(x):
    return x / (1.0 + jnp.exp(-x))


def _split_bf16(x, n):
    parts = []
    r = x
    for i in range(n):
        p = r.astype(BF16)
        parts.append(p)
        if i + 1 < n:
            r = r - p.astype(F32)
    return parts


def _dot_f32_lhs(x, rhs_exact, n):
    parts = _split_bf16(x, n)
    acc = _dot(parts[0], rhs_exact)
    for p in parts[1:]:
        acc = acc + _dot(p, rhs_exact)
    return acc


def _dot_f32_rhs(lhs_exact, x, n):
    parts = _split_bf16(x, n)
    acc = _dot(lhs_exact, parts[0])
    for p in parts[1:]:
        acc = acc + _dot(lhs_exact, p)
    return acc


def _rmsnorm(x, w):
    return x * lax.rsqrt(jnp.mean(x * x, axis=-1, keepdims=True) + EPS) * w


_Z_COLS = (0, SSD_WIDTH)
_XBC_COLS = (SSD_WIDTH, SSD_WIDTH + CONV_DIM)
_QKV_COLS = (_XBC_COLS[1], _XBC_COLS[1] + 3 * SB_WIDTH)
_P_COLS = (_QKV_COLS[1], D_IN_PERM - DT_PAD)
_DT_COLS = (D_IN_PERM - DT_PAD, D_IN_PERM)


def _layer_spec(shape, layer, grid_rank, single_buffer=False):
    zeros = (0,) * len(shape)
    index_map = {1: lambda i: (layer,) + zeros, 2: lambda i, j: (layer,) + zeros}[grid_rank]
    kwargs = {"pipeline_mode": pl.Buffered(1)} if single_buffer else {}
    return pl.BlockSpec((None,) + tuple(shape), index_map, **kwargs)


_PROJ_SEGS = (_Z_COLS, _XBC_COLS, _QKV_COLS, _P_COLS, _DT_COLS)


def _norm_inproj_kernel(x_ref, nw_ref, w_ref, *out_refs):
    hb = _rmsnorm(x_ref[...], nw_ref[...]).astype(BF16)
    for ref, (lo, hi) in zip(out_refs, _PROJ_SEGS):
        ref[...] = _dot(hb, w_ref[:, lo:hi])


def _norm_inproj(x2, norm_w, w_perm, layer):
    nt = x2.shape[0]
    row = lambda width: pl.BlockSpec((TM_PROJ, width), lambda i: (i, 0))
    widths = [hi - lo for lo, hi in _PROJ_SEGS]
    return pl.pallas_call(
        _norm_inproj_kernel,
        grid=(nt // TM_PROJ,),
        in_specs=[row(D_MODEL), _layer_spec((1, D_MODEL), layer, 1),
                  _layer_spec((D_MODEL, D_IN_PERM), layer, 1, single_buffer=True)],
        out_specs=[row(w) for w in widths],
        out_shape=[jax.ShapeDtypeStruct((nt, w), F32) for w in widths],
        compiler_params=pltpu.CompilerParams(
            dimension_semantics=("arbitrary",), vmem_limit_bytes=VMEM_LIMIT),
        name="norm_inproj",
    )(x2, norm_w, w_perm)


def _ssd_kernel(z_ref, xbc_ref, dt_ref, convw_ref, convb_ref, dtb_ref, alog_ref, dskip_ref,
                nw_ref, tri_ref, upper_ref, expand_ref, o_ref, ext_ref, state_ref):
    c = pl.program_id(1)

    @pl.when(c == 0)
    def _():
        ext_ref[:, 0:SUBLANES, :] = jnp.zeros((SSD_BATCH_TILE, SUBLANES, CONV_DIM), F32)
        state_ref[...] = jnp.zeros_like(state_ref)

    for e in range(SSD_BATCH_TILE):
        _ssd_chunk(z_ref.at[e], xbc_ref.at[e], dt_ref.at[e], convw_ref, convb_ref, dtb_ref,
                   alog_ref, dskip_ref, nw_ref, tri_ref, upper_ref, expand_ref, o_ref.at[e],
                   ext_ref.at[e], state_ref.at[e])


def _ssd_chunk(z_ref, xbc_ref, dt_ref, convw_ref, convb_ref, dtb_ref, alog_ref, dskip_ref,
               nw_ref, tri_ref, upper_ref, expand_ref, o_ref, ext_ref, state_ref):
    L = CHUNK
    ext_ref[SUBLANES:SUBLANES + L, :] = xbc_ref[...]
    full = ext_ref[...]
    u = convb_ref[...] + convw_ref[CONV_WIDTH - 1:CONV_WIDTH, :] * full[SUBLANES:, :]
    shifted = full
    for i in range(CONV_WIDTH - 2, -1, -1):
        shifted = pltpu.roll(shifted, 1, axis=0)
        u = u + convw_ref[i:i + 1, :] * shifted[SUBLANES:, :]
    ext_ref[0:SUBLANES, :] = ext_ref[L:L + SUBLANES, :]
    act = _silu(u)
    xs = act[:, 0:SSD_WIDTH]
    xs_b = xs.astype(BF16)
    bc = act[:, SSD_WIDTH:CONV_DIM].astype(BF16)

    dt = _softplus(dt_ref[...] + dtb_ref[...])
    a = -jnp.exp(alog_ref[...])
    d_a = dt * a

    row = lax.broadcasted_iota(jnp.int32, (L, L), 0)
    col = lax.broadcasted_iota(jnp.int32, (L, L), 1)
    causal = col <= row
    acum = _dot_f32_rhs(tri_ref[...], d_a, 3)
    d_a_t = jnp.transpose(d_a)[0:2 * SUBLANES, :]
    acum_t = _dot_f32_lhs(d_a_t, upper_ref[...], 3)
    dt_t = jnp.transpose(dt)[0:2 * SUBLANES, :]

    total = acum[L - 1:L, :]
    eac = jnp.exp(acum)
    f_state = dt * jnp.exp(total - acum)

    expand = expand_ref[...]
    eac_x = _dot_f32_lhs(eac, expand, 2)
    f_state_x = _dot_f32_lhs(f_state, expand, 2)

    lane = lax.broadcasted_iota(jnp.int32, (L, LANES), 1)
    low_half = lane < SSD_HEAD_DIM

    y_parts = []
    for g in range(SSD_GROUPS):
        b_g = bc[:, g * D_STATE:(g + 1) * D_STATE]
        c_off = SSD_GROUPS * D_STATE
        c_g = bc[:, c_off + g * D_STATE:c_off + (g + 1) * D_STATE]
        cb = lax.dot_general(c_g, b_g, (((1,), (1,)), ((), ())), preferred_element_type=F32)
        gw = SSD_WIDTH // SSD_GROUPS
        st = state_ref[g]
        y_off = _dot(c_g, st.astype(BF16)) * eac_x[:, g * gw:(g + 1) * gw]
        for j in range(2):
            lo = g * gw + j * LANES
            x_pair = xs_b[:, lo:lo + LANES]
            y_pair = None
            for k in range(2):
                h = (lo // SSD_HEAD_DIM) + k
                seg = acum[:, h:h + 1] - acum_t[h:h + 1, :]
                decay = jnp.exp(jnp.where(causal, seg, -jnp.inf))
                m = (cb * decay * dt_t[h:h + 1, :]).astype(BF16)
                x_h = jnp.where(low_half if k == 0 else jnp.logical_not(low_half), x_pair, 0)
                contrib = _dot(m, x_h)
                y_pair = contrib if y_pair is None else y_pair + contrib
            y_parts.append(y_pair + y_off[:, j * LANES:(j + 1) * LANES])
        xd = (xs[:, g * gw:(g + 1) * gw] * f_state_x[:, g * gw:(g + 1) * gw]).astype(BF16)
        s_add = lax.dot_general(b_g, xd, (((0,), (0,)), ((), ())), preferred_element_type=F32)
        state_ref[g] = st * eac_x[L - 1:L, g * gw:(g + 1) * gw] + s_add

    y = jnp.concatenate(y_parts, axis=1) + xs * dskip_ref[...]
    y = y * _silu(z_ref[...])
    o_ref[...] = _rmsnorm(y, nw_ref[...])


def _ssd(z, xbc, dt, conv_w, conv_b, dt_bias, a_log, d_skip_x, norm_w, consts, layer, batch,
         seq):
    nc = seq // CHUNK
    as_seq = lambda t: t.reshape(batch, seq, t.shape[-1])
    row = lambda width: pl.BlockSpec((SSD_BATCH_TILE, CHUNK, width), lambda b, c: (b, c, 0))
    const = lambda shape: pl.BlockSpec(shape, lambda b, c: (0, 0))
    spec = lambda shape: _layer_spec(shape, layer, 2)
    tri, upper, expand = consts
    return pl.pallas_call(
        _ssd_kernel,
        grid=(batch // SSD_BATCH_TILE, nc),
        in_specs=[row(SSD_WIDTH), row(CONV_DIM), row(DT_PAD),
                  spec((CONV_WIDTH, CONV_DIM)), spec((1, CONV_DIM)), spec((1, DT_PAD)),
                  spec((1, DT_PAD)), spec((1, SSD_WIDTH)), spec((1, SSD_WIDTH)),
                  const((CHUNK, CHUNK)), const((CHUNK, CHUNK)), const((DT_PAD, SSD_WIDTH))],
        out_specs=row(SSD_WIDTH),
        out_shape=jax.ShapeDtypeStruct((batch, seq, SSD_WIDTH), F32),
        scratch_shapes=[pltpu.VMEM((SSD_BATCH_TILE, CHUNK + SUBLANES, CONV_DIM), F32),
                        pltpu.VMEM((SSD_BATCH_TILE, SSD_GROUPS, D_STATE,
                                    SSD_WIDTH // SSD_GROUPS), F32)],
        compiler_params=pltpu.CompilerParams(
            dimension_semantics=("arbitrary", "arbitrary"), vmem_limit_bytes=VMEM_LIMIT),
        name="ssd_mixer",
    )(as_seq(z), as_seq(xbc), as_seq(dt), conv_w, conv_b, dt_bias, a_log, d_skip_x, norm_w, tri,
      upper, expand).reshape(batch * seq, SSD_WIDTH)


def _ssd_constants():
    idx = jnp.arange(CHUNK)
    tri = (idx[None, :] <= idx[:, None]).astype(BF16)
    upper = (idx[:, None] <= idx[None, :]).astype(BF16)
    expand = (jnp.arange(SSD_WIDTH)[None, :] // SSD_HEAD_DIM
              == jnp.arange(DT_PAD)[:, None]).astype(BF16)
    return tri, upper, expand


def _sb_kernel(q_ref, k_ref, v_ref, o_ref, kb_ref, vbm_ref, q4_ref, acc_ref, carry_ref):
    T = SB_BLOCK
    H = SB_HEADS
    qi = pl.program_id(1)
    head_of_lane = jnp.right_shift(lax.broadcasted_iota(jnp.int32, (1, SB_WIDTH), 1),
                                   HEAD_DIM_SHIFT)

    @pl.when(qi == 0)
    def _():
        kb_ref[...] = k_ref[...].astype(BF16)
        v = v_ref[...]
        for h in range(H):
            vbm_ref[h] = jnp.where(head_of_lane == h, v, 0.0).astype(BF16)

    q = q_ref[...] * (SB_HEAD_DIM ** -0.5)
    for h in range(H):
        q4_ref[h * T:(h + 1) * T, :] = jnp.where(head_of_lane == h, q, 0.0).astype(BF16)

    row = lax.broadcasted_iota(jnp.int32, (T, T), 0)
    col = lax.broadcasted_iota(jnp.int32, (T, T), 1)
    before = col < row
    tri = (row > col).astype(BF16)

    def scores(j):
        kb = kb_ref[pl.ds(pl.multiple_of(j * T, T), T), :]
        return lax.dot_general(q4_ref[...], kb, (((1,), (1,)), ((), ())),
                               preferred_element_type=F32)

    def weights(z4, carries, diagonal):
        ws = []
        new_carries = []
        for h in range(H):
            z = z4[h * T:(h + 1) * T, :]
            e = jnp.exp2(jnp.abs(z) * (-LOG2E))
            sp = jnp.maximum(z, 0.0) + jnp.log(1.0 + e)
            if diagonal:
                sp = jnp.where(before, sp, 0.0)
            arg = z - sp - _dot(sp.astype(BF16), tri)
            if carries is not None:
                arg = arg - jnp.sum(carries[h], axis=1, keepdims=True)
            w = jnp.exp2(arg * LOG2E)
            if diagonal:
                w = jnp.where(before, w, 0.0)
            ws.append(w.astype(BF16))
            part = sp[:, 0:LANES] + sp[:, LANES:2 * LANES]
            new_carries.append(part if carries is None else carries[h] + part)
        return tuple(ws), tuple(new_carries)

    def values(ws, j):
        start = pl.multiple_of(j * T, T)
        pv = _dot(ws[0], vbm_ref[0, pl.ds(start, T), :])
        for h in range(1, H):
            pv = pv + _dot(ws[h], vbm_ref[h, pl.ds(start, T), :])
        return pv

    def block(j, carries, diagonal):
        ws, carries = weights(scores(j), carries, diagonal)
        return values(ws, j), carries

    def save(pv, carries, first):
        if first:
            acc_ref[...] = pv
        else:
            acc_ref[...] += pv
        for h in range(H):
            carry_ref[h] = carries[h]

    def saved_carries():
        return tuple(carry_ref[h] for h in range(H))

    @pl.when(qi == 0)
    def _():
        pv, carries = block(qi, None, True)
        save(pv, carries, True)

    @pl.when(qi > 0)
    def _():
        pv0, carries = block(qi, None, True)
        pv1, carries = block(qi - 1, carries, False)
        save(pv0 + pv1, carries, True)

    rest = jnp.maximum(qi - 1, 0)
    odd = rest & 1

    @pl.when(odd == 1)
    def _():
        pv, carries = block(qi - 2, saved_carries(), False)
        save(pv, carries, False)

    def body(it, _):
        j = qi - 2 - odd - 2 * it
        pv0, carries = block(j, saved_carries(), False)
        pv1, carries = block(j - 1, carries, False)
        save(pv0 + pv1, carries, False)
        return 0

    lax.fori_loop(0, rest >> 1, body, 0)
    o_ref[...] = acc_ref[...]


def _sb_attention(qkv, batch, seq):
    nt = qkv.shape[0]
    nq = seq // SB_BLOCK
    return pl.pallas_call(
        _sb_kernel,
        grid=(batch, nq),
        in_specs=[pl.BlockSpec((SB_BLOCK, SB_WIDTH), lambda b, i: (b * nq + i, 0)),
                  pl.BlockSpec((seq, SB_WIDTH), lambda b, i: (b, 1)),
                  pl.BlockSpec((seq, SB_WIDTH), lambda b, i: (b, 2))],
        out_specs=pl.BlockSpec((SB_BLOCK, SB_WIDTH), lambda b, i: (b * nq + i, 0)),
        out_shape=jax.ShapeDtypeStruct((nt, SB_WIDTH), F32),
        scratch_shapes=[pltpu.VMEM((seq, SB_WIDTH), BF16),
                        pltpu.VMEM((SB_HEADS, seq, SB_WIDTH), BF16),
                        pltpu.VMEM((SB_HEADS * SB_BLOCK, SB_WIDTH), BF16),
                        pltpu.VMEM((SB_BLOCK, SB_WIDTH), F32),
                        pltpu.VMEM((SB_HEADS, SB_BLOCK, LANES), F32)],
        compiler_params=pltpu.CompilerParams(
            dimension_semantics=("arbitrary", "arbitrary"), vmem_limit_bytes=VMEM_LIMIT),
        name="sb_attention",
    )(qkv, qkv, qkv)


def _pool_out_ffn_kernel(x_ref, yssd_ref, ysb_ref, p_ref, halo_ref, poolw_ref, poolb_ref,
                         pscale_ref, wout_ref, n2_ref, wg_ref, wu_ref, wd_ref, fn_ref,
                         o_ref, buf_a, buf_b, *, tiles_per_seq, final_norm):
    TM = TM_FFN
    i = pl.program_id(0)
    t0 = (i % tiles_per_seq) * TM

    top = SUBLANES + POOL_HALO
    p_cur = p_ref[...]
    zeros8 = jnp.zeros((SUBLANES, POOL_WIDTH), F32)
    buf_a[0:SUBLANES, :] = zeros8
    buf_b[0:SUBLANES, :] = zeros8
    buf_a[SUBLANES:top, :] = jnp.where(t0 == 0, 0.0, halo_ref[...])
    buf_a[top:top + TM, :] = p_cur
    lane = lax.broadcasted_iota(jnp.int32, (TM, POOL_WIDTH), 1)
    group = jnp.right_shift(lane, HEAD_DIM_SHIFT)
    n_rows = TM + POOL_HALO
    wsum = None
    src, dst = buf_a, buf_b
    for gi, win in enumerate(POOL_WINDOWS):
        shift = win // 2
        s = src[SUBLANES:SUBLANES + n_rows, :] + src[SUBLANES - shift:SUBLANES - shift + n_rows, :]
        dst[SUBLANES:SUBLANES + n_rows, :] = s
        tile_sum = s[POOL_HALO:, :]
        wsum = tile_sum if wsum is None else jnp.where(group >= gi, tile_sum, wsum)
        src, dst = dst, src
    pos1 = t0 + 1 + lax.broadcasted_iota(jnp.int32, (TM, POOL_WIDTH), 0)
    win_lane = jnp.left_shift(2, group)
    count = jnp.minimum(pos1, win_lane).astype(F32)
    pooled = wsum / count - p_cur
    mixed = _dot(pooled.astype(BF16), poolw_ref[...]) + poolb_ref[...]
    y_pool = mixed * pscale_ref[...]

    a0, a1 = SSD_WIDTH, SSD_WIDTH + SB_WIDTH
    o_ref[...] = (x_ref[...]
                  + _dot(yssd_ref[...].astype(BF16), wout_ref[0:a0, :])
                  + _dot(ysb_ref[...].astype(BF16), wout_ref[a0:a1, :])
                  + _dot(y_pool.astype(BF16), wout_ref[a1:, :]))

    x1 = o_ref[...]
    hb = _rmsnorm(x1, n2_ref[...]).astype(BF16)
    acc = x1
    for lo, hi in ((0, FF_SPLIT), (FF_SPLIT, D_FF)):
        gate = _dot(hb, wg_ref[:, lo:hi])
        up = _dot(hb, wu_ref[:, lo:hi])
        act = (_silu(gate) * up).astype(BF16)
        acc = acc + _dot(act, wd_ref[lo:hi, :])
    if final_norm:
        acc = _rmsnorm(acc, fn_ref[...])
    o_ref[...] = acc


def _pool_out_ffn(x2, y_ssd, y_sb, p, pool_w_bd, pool_b, pool_scale, w_out, norm2_w, w_gate,
                  w_up, w_down, final_w, layer, seq, final_norm):
    nt = x2.shape[0]
    tiles_per_seq = seq // TM_FFN
    halo_per_tile = TM_FFN // POOL_HALO
    row = lambda width: pl.BlockSpec((TM_FFN, width), lambda i: (i, 0))
    spec = lambda shape: _layer_spec(shape, layer, 1, single_buffer=True)
    halo = pl.BlockSpec((POOL_HALO, POOL_WIDTH),
                        lambda i: (jnp.maximum(i * halo_per_tile - 1, 0), 0))
    kern = functools.partial(_pool_out_ffn_kernel, tiles_per_seq=tiles_per_seq,
                             final_norm=final_norm)
    buf_rows = SUBLANES + POOL_HALO + TM_FFN
    return pl.pallas_call(
        kern,
        grid=(nt // TM_FFN,),
        in_specs=[row(D_MODEL), row(SSD_WIDTH), row(SB_WIDTH), row(POOL_WIDTH), halo,
                  spec((POOL_WIDTH, POOL_WIDTH)), spec((1, POOL_WIDTH)), spec((1, POOL_WIDTH)),
                  spec((D_MODEL, D_MODEL)), spec((1, D_MODEL)),
                  spec((D_MODEL, D_FF)), spec((D_MODEL, D_FF)), spec((D_FF, D_MODEL)),
                  pl.BlockSpec((1, D_MODEL), lambda i: (0, 0))],
        out_specs=row(D_MODEL),
        out_shape=jax.ShapeDtypeStruct((nt, D_MODEL), F32),
        scratch_shapes=[pltpu.VMEM((buf_rows, POOL_WIDTH), F32),
                        pltpu.VMEM((buf_rows, POOL_WIDTH), F32)],
        compiler_params=pltpu.CompilerParams(
            dimension_semantics=("arbitrary",), vmem_limit_bytes=VMEM_LIMIT),
        name="pool_out_ffn",
    )(x2, y_ssd, y_sb, p, p, pool_w_bd, pool_b, pool_scale, w_out, norm2_w, w_gate, w_up,
      w_down, final_w)


def _permute_w_in(w_in):
    s0 = SSD_WIDTH + CONV_DIM
    s1 = s0 + SSD_HEADS
    pad = jnp.zeros(w_in.shape[:-1] + (DT_PAD - SSD_HEADS,), w_in.dtype)
    return jnp.concatenate([w_in[..., :s0], w_in[..., s1:], w_in[..., s0:s1], pad], axis=-1)


def _block_diag(pool_w):
    depth, g, c, d = pool_w.shape
    eye = jnp.eye(g, dtype=pool_w.dtype)
    return (pool_w[:, :, :, None, :] * eye[None, :, None, :, None]).reshape(depth, g * c, g * d)


def _rows(v, width=None):
    if width is not None and width > v.shape[-1]:
        v = jnp.pad(v, ((0, 0), (0, width - v.shape[-1])))
    return v[:, None, :]


def kernel(x, norm1_w, w_in, conv_w, conv_b, dt_bias, a_log, d_skip, ssd_norm_w, pool_w, pool_b,
           pool_scale, w_out, norm2_w, w_gate, w_up, w_down, final_norm_w):
    batch, seq, d_model = x.shape
    x2 = x.reshape(batch * seq, d_model)
    w_in_p = _permute_w_in(w_in.astype(BF16))
    w_out, w_gate, w_up, w_down = (w.astype(BF16) for w in (w_out, w_gate, w_up, w_down))
    pool_w_bd = _block_diag(pool_w).astype(BF16)
    norm1_r, norm2_r, conv_b_r = _rows(norm1_w), _rows(norm2_w), _rows(conv_b)
    dt_bias_r, a_log_r = _rows(dt_bias, DT_PAD), _rows(a_log, DT_PAD)
    d_skip_r = _rows(jnp.repeat(d_skip, SSD_HEAD_DIM, axis=1))
    ssd_norm_r = _rows(ssd_norm_w)
    pool_b_r = _rows(pool_b.reshape(DEPTH, POOL_WIDTH))
    pool_scale_r = _rows(pool_scale)
    final_w = final_norm_w.reshape(1, d_model)
    ssd_consts = _ssd_constants()
    for layer in range(DEPTH):
        z, xbc, qkv, p, dt = _norm_inproj(x2, norm1_r, w_in_p, layer)
        y_ssd = _ssd(z, xbc, dt, conv_w, conv_b_r, dt_bias_r, a_log_r, d_skip_r, ssd_norm_r,
                     ssd_consts, layer, batch, seq)
        y_sb = _sb_attention(qkv, batch, seq)
        x2 = _pool_out_ffn(x2, y_ssd, y_sb, p, pool_w_bd, pool_b_r, pool_scale_r, w_out, norm2_r,
                           w_gate, w_up, w_down, final_w, layer, seq,
                           final_norm=(layer == DEPTH - 1))
    return x2.reshape(batch, seq, d_model)
```

```python
import functools

import jax
import jax.numpy as jnp
from jax import lax
from jax.experimental import pallas as pl
from jax.experimental.pallas import tpu as pltpu

F32 = jnp.float32
BF16 = jnp.bfloat16

D_MODEL = 1024
DEPTH = 4
SSD_WIDTH = 512
SSD_HEADS = 8
SSD_HEAD_DIM = 64
SSD_GROUPS = 2
D_STATE = 128
CONV_WIDTH = 4
CHUNK = 128
CONV_DIM = SSD_WIDTH + 2 * SSD_GROUPS * D_STATE
SB_WIDTH = 256
SB_HEADS = 4
SB_HEAD_DIM = 64
POOL_WINDOWS = (2, 4, 8, 16)
POOL_WIDTH = 256
POOL_GROUP_DIM = 64
D_FF = 2816
EPS = 1e-6
HEAD_DIM_SHIFT = 6
LOG2E = 1.4426950408889634

LANES = 128
SUBLANES = 8
DT_PAD = LANES
VMEM_LIMIT = 56 * 1024 * 1024

TM_PROJ = 512
TM_FFN = 512
FF_SPLIT = 1536
SB_BLOCK = 256
SSD_BATCH_TILE = 4
POOL_HALO = 16


def _dot(a, b):
    return jnp.dot(a, b, preferred_element_type=F32)


def _softplus(x):
    return jnp.maximum(x, 0.0) + jnp.log1p(jnp.exp(-jnp.abs(x)))


def _silu(x):
    return x / (1.0 + jnp.exp(-x))


def _split_bf16(x, n):
    parts = []
    r = x
    for i in range(n):
        p = r.astype(BF16)
        parts.append(p)
        if i + 1 < n:
            r = r - p.astype(F32)
    return parts


def _dot_f32_lhs(x, rhs_exact, n):
    parts = _split_bf16(x, n)
    acc = _dot(parts[0], rhs_exact)
    for p in parts[1:]:
        acc = acc + _dot(p, rhs_exact)
    return acc


def _dot_f32_rhs(lhs_exact, x, n):
    parts = _split_bf16(x, n)
    acc = _dot(lhs_exact, parts[0])
    for p in parts[1:]:
        acc = acc + _dot(lhs_exact, p)
    return acc


def _rmsnorm(x, w):
    return x * lax.rsqrt(jnp.mean(x * x, axis=-1, keepdims=True) + EPS) * w


D_IN_PROJ = SSD_WIDTH + CONV_DIM + SSD_HEADS + 3 * SB_WIDTH + POOL_WIDTH
_Z_COLS = (0, SSD_WIDTH)
_XBC_COLS = (SSD_WIDTH, SSD_WIDTH + CONV_DIM)
_TAIL_COLS = (_XBC_COLS[1], D_IN_PROJ)


def _layer_spec(shape, layer, grid_rank, single_buffer=False):
    zeros = (0,) * len(shape)
    index_map = {1: lambda i: (layer,) + zeros, 2: lambda i, j: (layer,) + zeros}[grid_rank]
    kwargs = {"pipeline_mode": pl.Buffered(1)} if single_buffer else {}
    return pl.BlockSpec((None,) + tuple(shape), index_map, **kwargs)


def _norm_inproj_kernel(x_ref, nw_ref, w_ref, z_ref, xbc_ref, qkv_ref, p_ref, dt_ref):
    hb = _rmsnorm(x_ref[...], nw_ref[...]).astype(BF16)
    z_ref[...] = _dot(hb, w_ref[:, _Z_COLS[0]:_Z_COLS[1]])
    xbc_ref[...] = _dot(hb, w_ref[:, _XBC_COLS[0]:_XBC_COLS[1]])
    tail = _dot(hb, w_ref[:, _TAIL_COLS[0]:_TAIL_COLS[1]])
    dt_ref[...] = tail[:, 0:DT_PAD]
    qkv_ref[...] = tail[:, SSD_HEADS:SSD_HEADS + 3 * SB_WIDTH]
    p_ref[...] = tail[:, SSD_HEADS + 3 * SB_WIDTH:]


def _norm_inproj(x2, norm_w, w_in, layer):
    nt = x2.shape[0]
    row = lambda width: pl.BlockSpec((TM_PROJ, width), lambda i: (i, 0))
    widths = [SSD_WIDTH, CONV_DIM, 3 * SB_WIDTH, POOL_WIDTH, DT_PAD]
    return pl.pallas_call(
        _norm_inproj_kernel,
        grid=(nt // TM_PROJ,),
        in_specs=[row(D_MODEL), _layer_spec((1, D_MODEL), layer, 1),
                  _layer_spec((D_MODEL, D_IN_PROJ), layer, 1, single_buffer=True)],
        out_specs=[row(w) for w in widths],
        out_shape=[jax.ShapeDtypeStruct((nt, w), F32) for w in widths],
        compiler_params=pltpu.CompilerParams(
            dimension_semantics=("arbitrary",), vmem_limit_bytes=VMEM_LIMIT),
        name="norm_inproj",
    )(x2, norm_w, w_in)


def _ssd_kernel(z_ref, xbc_ref, dt_ref, convw_ref, convb_ref, dtb_ref, alog_ref, dskip_ref,
                nw_ref, tri_ref, upper_ref, expand_ref, o_ref, ext_ref, state_ref):
    c = pl.program_id(1)

    @pl.when(c == 0)
    def _():
        ext_ref[:, 0:SUBLANES, :] = jnp.zeros((SSD_BATCH_TILE, SUBLANES, CONV_DIM), F32)
        state_ref[...] = jnp.zeros_like(state_ref)

    for e in range(SSD_BATCH_TILE):
        _ssd_chunk(z_ref.at[e], xbc_ref.at[e], dt_ref.at[e], convw_ref, convb_ref, dtb_ref,
                   alog_ref, dskip_ref, nw_ref, tri_ref, upper_ref, expand_ref, o_ref.at[e],
                   ext_ref.at[e], state_ref.at[e])


def _ssd_chunk(z_ref, xbc_ref, dt_ref, convw_ref, convb_ref, dtb_ref, alog_ref, dskip_ref,
               nw_ref, tri_ref, upper_ref, expand_ref, o_ref, ext_ref, state_ref):
    L = CHUNK
    ext_ref[SUBLANES:SUBLANES + L, :] = xbc_ref[...]
    full = ext_ref[...]
    u = convb_ref[...] + convw_ref[CONV_WIDTH - 1:CONV_WIDTH, :] * full[SUBLANES:, :]
    shifted = full
    for i in range(CONV_WIDTH - 2, -1, -1):
        shifted = pltpu.roll(shifted, 1, axis=0)
        u = u + convw_ref[i:i + 1, :] * shifted[SUBLANES:, :]
    ext_ref[0:SUBLANES, :] = ext_ref[L:L + SUBLANES, :]
    act = _silu(u)
    xs = act[:, 0:SSD_WIDTH]
    xs_b = xs.astype(BF16)
    bc = act[:, SSD_WIDTH:CONV_DIM].astype(BF16)

    dt = _softplus(dt_ref[...] + dtb_ref[...])
    a = -jnp.exp(alog_ref[...])
    d_a = dt * a

    row = lax.broadcasted_iota(jnp.int32, (L, L), 0)
    col = lax.broadcasted_iota(jnp.int32, (L, L), 1)
    causal = col <= row
    acum = _dot_f32_rhs(tri_ref[...], d_a, 3)
    d_a_t = jnp.transpose(d_a)[0:2 * SUBLANES, :]
    acum_t = _dot_f32_lhs(d_a_t, upper_ref[...], 3)
    dt_t = jnp.transpose(dt)[0:2 * SUBLANES, :]

    total = acum[L - 1:L, :]
    eac = jnp.exp(acum)
    f_state = dt * jnp.exp(total - acum)

    expand = expand_ref[...]
    eac_x = _dot_f32_lhs(eac, expand, 2)
    f_state_x = _dot_f32_lhs(f_state, expand, 2)

    lane = lax.broadcasted_iota(jnp.int32, (L, LANES), 1)
    low_half = lane < SSD_HEAD_DIM

    y_parts = []
    for g in range(SSD_GROUPS):
        b_g = bc[:, g * D_STATE:(g + 1) * D_STATE]
        c_off = SSD_GROUPS * D_STATE
        c_g = bc[:, c_off + g * D_STATE:c_off + (g + 1) * D_STATE]
        cb = lax.dot_general(c_g, b_g, (((1,), (1,)), ((), ())), preferred_element_type=F32)
        gw = SSD_WIDTH // SSD_GROUPS
        st = state_ref[g]
        y_off = _dot(c_g, st.astype(BF16)) * eac_x[:, g * gw:(g + 1) * gw]
        for j in range(2):
            lo = g * gw + j * LANES
            x_pair = xs_b[:, lo:lo + LANES]
            y_pair = None
            for k in range(2):
                h = (lo // SSD_HEAD_DIM) + k
                seg = acum[:, h:h + 1] - acum_t[h:h + 1, :]
                decay = jnp.exp(jnp.where(causal, seg, -jnp.inf))
                m = (cb * decay * dt_t[h:h + 1, :]).astype(BF16)
                x_h = jnp.where(low_half if k == 0 else jnp.logical_not(low_half), x_pair, 0)
                contrib = _dot(m, x_h)
                y_pair = contrib if y_pair is None else y_pair + contrib
            y_parts.append(y_pair + y_off[:, j * LANES:(j + 1) * LANES])
        xd = (xs[:, g * gw:(g + 1) * gw] * f_state_x[:, g * gw:(g + 1) * gw]).astype(BF16)
        s_add = lax.dot_general(b_g, xd, (((0,), (0,)), ((), ())), preferred_element_type=F32)
        state_ref[g] = st * eac_x[L - 1:L, g * gw:(g + 1) * gw] + s_add

    y = jnp.concatenate(y_parts, axis=1) + xs * dskip_ref[...]
    y = y * _silu(z_ref[...])
    o_ref[...] = _rmsnorm(y, nw_ref[...])


def _ssd(z, xbc, dt, conv_w, conv_b, dt_bias, a_log, d_skip_x, norm_w, consts, layer, batch,
         seq):
    nc = seq // CHUNK
    as_seq = lambda t: t.reshape(batch, seq, t.shape[-1])
    row = lambda width: pl.BlockSpec((SSD_BATCH_TILE, CHUNK, width), lambda b, c: (b, c, 0))
    const = lambda shape: pl.BlockSpec(shape, lambda b, c: (0, 0))
    spec = lambda shape: _layer_spec(shape, layer, 2)
    tri, upper, expand = consts
    return pl.pallas_call(
        _ssd_kernel,
        grid=(batch // SSD_BATCH_TILE, nc),
        in_specs=[row(SSD_WIDTH), row(CONV_DIM), row(DT_PAD),
                  spec((CONV_WIDTH, CONV_DIM)), spec((1, CONV_DIM)), spec((1, DT_PAD)),
                  spec((1, DT_PAD)), spec((1, SSD_WIDTH)), spec((1, SSD_WIDTH)),
                  const((CHUNK, CHUNK)), const((CHUNK, CHUNK)), const((DT_PAD, SSD_WIDTH))],
        out_specs=row(SSD_WIDTH),
        out_shape=jax.ShapeDtypeStruct((batch, seq, SSD_WIDTH), F32),
        scratch_shapes=[pltpu.VMEM((SSD_BATCH_TILE, CHUNK + SUBLANES, CONV_DIM), F32),
                        pltpu.VMEM((SSD_BATCH_TILE, SSD_GROUPS, D_STATE,
                                    SSD_WIDTH // SSD_GROUPS), F32)],
        compiler_params=pltpu.CompilerParams(
            dimension_semantics=("arbitrary", "arbitrary"), vmem_limit_bytes=VMEM_LIMIT),
        name="ssd_mixer",
    )(as_seq(z), as_seq(xbc), as_seq(dt), conv_w, conv_b, dt_bias, a_log, d_skip_x, norm_w, tri,
      upper, expand).reshape(batch * seq, SSD_WIDTH)


def _ssd_constants():
    idx = jnp.arange(CHUNK)
    tri = (idx[None, :] <= idx[:, None]).astype(BF16)
    upper = (idx[:, None] <= idx[None, :]).astype(BF16)
    expand = (jnp.arange(SSD_WIDTH)[None, :] // SSD_HEAD_DIM
              == jnp.arange(DT_PAD)[:, None]).astype(BF16)
    return tri, upper, expand


def _sb_kernel(q_ref, k_ref, v_ref, o_ref, kb_ref, vbm_ref, qs_ref, acc_ref, carry_ref):
    T = SB_BLOCK
    H = SB_HEADS
    U = 2 * H
    si = pl.program_id(1)
    head_of_lane = jnp.right_shift(lax.broadcasted_iota(jnp.int32, (1, SB_WIDTH), 1),
                                   HEAD_DIM_SHIFT)

    @pl.when(si == 0)
    def _():
        kb_ref[...] = k_ref[...].astype(BF16)
        v = v_ref[...]
        for h in range(H):
            vbm_ref[h] = jnp.where(head_of_lane == h, v, 0.0).astype(BF16)

    for u in range(U):
        q = q_ref[(u // H) * T:(u // H + 1) * T, :] * (SB_HEAD_DIM ** -0.5)
        qs_ref[u * T:(u + 1) * T, :] = jnp.where(head_of_lane == u % H, q, 0.0).astype(BF16)

    row = lax.broadcasted_iota(jnp.int32, (T, T), 0)
    col = lax.broadcasted_iota(jnp.int32, (T, T), 1)
    before = col < row
    tri = (row > col).astype(BF16)

    def scores(u0, n_units, j):
        kb = kb_ref[pl.ds(pl.multiple_of(j * T, T), T), :]
        return lax.dot_general(qs_ref[u0 * T:(u0 + n_units) * T, :], kb,
                               (((1,), (1,)), ((), ())), preferred_element_type=F32)

    def weights(zs, carries, diagonal):
        ws = []
        new_carries = []
        for i in range(zs.shape[0] // T):
            z = zs[i * T:(i + 1) * T, :]
            e = jnp.exp2(jnp.abs(z) * (-LOG2E))
            sp = jnp.maximum(z, 0.0) + jnp.log(1.0 + e)
            if diagonal:
                sp = jnp.where(before, sp, 0.0)
            arg = z - sp - _dot(sp.astype(BF16), tri)
            if carries is not None:
                arg = arg - jnp.sum(carries[i], axis=1, keepdims=True)
            w = jnp.exp2(arg * LOG2E)
            if diagonal:
                w = jnp.where(before, w, 0.0)
            ws.append(w.astype(BF16))
            part = sp[:, 0:LANES] + sp[:, LANES:2 * LANES]
            new_carries.append(part if carries is None else carries[i] + part)
        return ws, new_carries

    def values(ws, j):
        start = pl.multiple_of(j * T, T)
        n_q = len(ws) // H
        pv = None
        for h in range(H):
            lhs = ws[h] if n_q == 1 else jnp.concatenate([ws[h], ws[H + h]], axis=0)
            t = _dot(lhs, vbm_ref[h, pl.ds(start, T), :])
            pv = t if pv is None else pv + t
        return pv

    ja = 2 * si
    ws_a, carries_a = weights(scores(0, H, ja), None, True)
    ws_b1, carries_b = weights(scores(H, H, ja + 1), None, True)
    ws_b0, carries_b = weights(scores(H, H, ja), carries_b, False)
    acc_ref[0:T, :] = values(ws_a, ja)
    acc_ref[T:2 * T, :] = values(ws_b1, ja + 1) + values(ws_b0, ja)
    for u, c in enumerate(carries_a + carries_b):
        carry_ref[u] = c

    def body(it, _):
        j = ja - 1 - 2 * it
        ws1, carries = weights(scores(0, U, j), [carry_ref[u] for u in range(U)], False)
        ws0, carries = weights(scores(0, U, j - 1), carries, False)
        acc_ref[...] += values(ws1, j) + values(ws0, j - 1)
        for u in range(U):
            carry_ref[u] = carries[u]
        return 0

    lax.fori_loop(0, si, body, 0)
    o_ref[...] = acc_ref[...]


def _sb_attention(qkv, batch, seq):
    nt = qkv.shape[0]
    rows = 2 * SB_BLOCK
    ns = seq // rows
    return pl.pallas_call(
        _sb_kernel,
        grid=(batch, ns),
        in_specs=[pl.BlockSpec((rows, SB_WIDTH), lambda b, i: (b * ns + i, 0)),
                  pl.BlockSpec((seq, SB_WIDTH), lambda b, i: (b, 1)),
                  pl.BlockSpec((seq, SB_WIDTH), lambda b, i: (b, 2))],
        out_specs=pl.BlockSpec((rows, SB_WIDTH), lambda b, i: (b * ns + i, 0)),
        out_shape=jax.ShapeDtypeStruct((nt, SB_WIDTH), F32),
        scratch_shapes=[pltpu.VMEM((seq, SB_WIDTH), BF16),
                        pltpu.VMEM((SB_HEADS, seq, SB_WIDTH), BF16),
                        pltpu.VMEM((2 * SB_HEADS * SB_BLOCK, SB_WIDTH), BF16),
                        pltpu.VMEM((rows, SB_WIDTH), F32),
                        pltpu.VMEM((2 * SB_HEADS, SB_BLOCK, LANES), F32)],
        compiler_params=pltpu.CompilerParams(
            dimension_semantics=("arbitrary", "arbitrary"), vmem_limit_bytes=VMEM_LIMIT),
        name="sb_attention",
    )(qkv, qkv, qkv)


def _pool_out_ffn_kernel(x_ref, yssd_ref, ysb_ref, p_ref, halo_ref, poolw_ref, poolb_ref,
                         pscale_ref, wout_ref, n2_ref, wg_ref, wu_ref, wd_ref, fn_ref,
                         o_ref, buf_a, buf_b, *, tiles_per_seq, final_norm):
    TM = TM_FFN
    i = pl.program_id(0)
    t0 = (i % tiles_per_seq) * TM

    top = SUBLANES + POOL_HALO
    p_cur = p_ref[...]
    zeros8 = jnp.zeros((SUBLANES, POOL_WIDTH), F32)
    buf_a[0:SUBLANES, :] = zeros8
    buf_b[0:SUBLANES, :] = zeros8
    buf_a[SUBLANES:top, :] = jnp.where(t0 == 0, 0.0, halo_ref[...])
    buf_a[top:top + TM, :] = p_cur
    lane = lax.broadcasted_iota(jnp.int32, (TM, POOL_WIDTH), 1)
    group = jnp.right_shift(lane, HEAD_DIM_SHIFT)
    n_rows = TM + POOL_HALO
    wsum = None
    src, dst = buf_a, buf_b
    for gi, win in enumerate(POOL_WINDOWS):
        shift = win // 2
        s = src[SUBLANES:SUBLANES + n_rows, :] + src[SUBLANES - shift:SUBLANES - shift + n_rows, :]
        dst[SUBLANES:SUBLANES + n_rows, :] = s
        tile_sum = s[POOL_HALO:, :]
        wsum = tile_sum if wsum is None else jnp.where(group >= gi, tile_sum, wsum)
        src, dst = dst, src
    pos1 = t0 + 1 + lax.broadcasted_iota(jnp.int32, (TM, POOL_WIDTH), 0)
    win_lane = jnp.left_shift(2, group)
    count = jnp.minimum(pos1, win_lane).astype(F32)
    pooled = wsum / count - p_cur
    mixed = _dot(pooled.astype(BF16), poolw_ref[...]) + poolb_ref[...]
    y_pool = mixed * pscale_ref[...]

    a0, a1 = SSD_WIDTH, SSD_WIDTH + SB_WIDTH
    o_ref[...] = (x_ref[...]
                  + _dot(yssd_ref[...].astype(BF16), wout_ref[0:a0, :])
                  + _dot(ysb_ref[...].astype(BF16), wout_ref[a0:a1, :])
                  + _dot(y_pool.astype(BF16), wout_ref[a1:, :]))

    x1 = o_ref[...]
    hb = _rmsnorm(x1, n2_ref[...]).astype(BF16)
    acc = x1
    for lo, hi in ((0, FF_SPLIT), (FF_SPLIT, D_FF)):
        gate = _dot(hb, wg_ref[:, lo:hi])
        up = _dot(hb, wu_ref[:, lo:hi])
        act = (_silu(gate) * up).astype(BF16)
        acc = acc + _dot(act, wd_ref[lo:hi, :])
    if final_norm:
        acc = _rmsnorm(acc, fn_ref[...])
    o_ref[...] = acc


def _pool_out_ffn(x2, y_ssd, y_sb, p, pool_w_bd, pool_b, pool_scale, w_out, norm2_w, w_gate,
                  w_up, w_down, final_w, layer, seq, final_norm):
    nt = x2.shape[0]
    tiles_per_seq = seq // TM_FFN
    halo_per_tile = TM_FFN // POOL_HALO
    row = lambda width: pl.BlockSpec((TM_FFN, width), lambda i: (i, 0))
    spec = lambda shape: _layer_spec(shape, layer, 1, single_buffer=True)
    halo = pl.BlockSpec((POOL_HALO, POOL_WIDTH),
                        lambda i: (jnp.maximum(i * halo_per_tile - 1, 0), 0))
    kern = functools.partial(_pool_out_ffn_kernel, tiles_per_seq=tiles_per_seq,
                             final_norm=final_norm)
    buf_rows = SUBLANES + POOL_HALO + TM_FFN
    return pl.pallas_call(
        kern,
        grid=(nt // TM_FFN,),
        in_specs=[row(D_MODEL), row(SSD_WIDTH), row(SB_WIDTH), row(POOL_WIDTH), halo,
                  spec((POOL_WIDTH, POOL_WIDTH)), spec((1, POOL_WIDTH)), spec((1, POOL_WIDTH)),
                  spec((D_MODEL, D_MODEL)), spec((1, D_MODEL)),
                  spec((D_MODEL, D_FF)), spec((D_MODEL, D_FF)), spec((D_FF, D_MODEL)),
                  pl.BlockSpec((1, D_MODEL), lambda i: (0, 0))],
        out_specs=row(D_MODEL),
        out_shape=jax.ShapeDtypeStruct((nt, D_MODEL), F32),
        scratch_shapes=[pltpu.VMEM((buf_rows, POOL_WIDTH), F32),
                        pltpu.VMEM((buf_rows, POOL_WIDTH), F32)],
        compiler_params=pltpu.CompilerParams(
            dimension_semantics=("arbitrary",), vmem_limit_bytes=VMEM_LIMIT),
        name="pool_out_ffn",
    )(x2, y_ssd, y_sb, p, p, pool_w_bd, pool_b, pool_scale, w_out, norm2_w, w_gate, w_up,
      w_down, final_w)


def _block_diag(pool_w):
    depth, g, c, d = pool_w.shape
    eye = jnp.eye(g, dtype=pool_w.dtype)
    return (pool_w[:, :, :, None, :] * eye[None, :, None, :, None]).reshape(depth, g * c, g * d)


def _rows(v, width=None):
    if width is not None and width > v.shape[-1]:
        v = jnp.pad(v, ((0, 0), (0, width - v.shape[-1])))
    return v[:, None, :]


def kernel(x, norm1_w, w_in, conv_w, conv_b, dt_bias, a_log, d_skip, ssd_norm_w, pool_w, pool_b,
           pool_scale, w_out, norm2_w, w_gate, w_up, w_down, final_norm_w):
    batch, seq, d_model = x.shape
    x2 = x.reshape(batch * seq, d_model)
    w_in = w_in.astype(BF16)
    w_out, w_gate, w_up, w_down = (w.astype(BF16) for w in (w_out, w_gate, w_up, w_down))
    pool_w_bd = _block_diag(pool_w).astype(BF16)
    norm1_r, norm2_r, conv_b_r = _rows(norm1_w), _rows(norm2_w), _rows(conv_b)
    dt_bias_r, a_log_r = _rows(dt_bias, DT_PAD), _rows(a_log, DT_PAD)
    d_skip_r = _rows(jnp.repeat(d_skip, SSD_HEAD_DIM, axis=1))
    ssd_norm_r = _rows(ssd_norm_w)
    pool_b_r = _rows(pool_b.reshape(DEPTH, POOL_WIDTH))
    pool_scale_r = _rows(pool_scale)
    final_w = final_norm_w.reshape(1, d_model)
    ssd_consts = _ssd_constants()
    for layer in range(DEPTH):
        z, xbc, qkv, p, dt = _norm_inproj(x2, norm1_r, w_in, layer)
        y_ssd = _ssd(z, xbc, dt, conv_w, conv_b_r, dt_bias_r, a_log_r, d_skip_r, ssd_norm_r,
                     ssd_consts, layer, batch, seq)
        y_sb = _sb_attention(qkv, batch, seq)
        x2 = _pool_out_ffn(x2, y_ssd, y_sb, p, pool_w_bd, pool_b_r, pool_scale_r, w_out, norm2_r,
                           w_gate, w_up, w_down, final_w, layer, seq,
                           final_norm=(layer == DEPTH - 1))
    return x2.reshape(batch, seq, d_model)
```

```python
import functools

import jax
import jax.numpy as jnp
from jax import lax
from jax.experimental import pallas as pl
from jax.experimental.pallas import tpu as pltpu

F32 = jnp.float32
BF16 = jnp.bfloat16

D_MODEL = 1024
DEPTH = 4
SSD_WIDTH = 512
SSD_HEADS = 8
SSD_HEAD_DIM = 64
SSD_GROUPS = 2
D_STATE = 128
CONV_WIDTH = 4
CHUNK = 128
CONV_DIM = SSD_WIDTH + 2 * SSD_GROUPS * D_STATE
SB_WIDTH = 256
SB_HEADS = 4
SB_HEAD_DIM = 64
POOL_WINDOWS = (2, 4, 8, 16)
POOL_WIDTH = 256
POOL_GROUP_DIM = 64
D_FF = 2816
EPS = 1e-6
HEAD_DIM_SHIFT = 6
LOG2E = 1.4426950408889634

LANES = 128
SUBLANES = 8
DT_PAD = LANES
VMEM_LIMIT = 56 * 1024 * 1024

TM_PROJ = 1024
TM_FFN = 512
FF_SPLIT = 1536
SB_BLOCK = 256
SSD_BATCH_TILE = 8
POOL_HALO = 16


def _dot(a, b):
    return jnp.dot(a, b, preferred_element_type=F32)


def _softplus(x):
    return jnp.maximum(x, 0.0) + jnp.log1p(jnp.exp(-jnp.abs(x)))


def _silu(x):
    h = 0.5 * x
    return h + h * jnp.tanh(h)


def _split_bf16(x, n):
    parts = []
    r = x
    for i in range(n):
        p = r.astype(BF16)
        parts.append(p)
        if i + 1 < n:
            r = r - p.astype(F32)
    return parts


def _dot_f32_lhs(x, rhs_exact, n):
    parts = _split_bf16(x, n)
    acc = _dot(parts[0], rhs_exact)
    for p in parts[1:]:
        acc = acc + _dot(p, rhs_exact)
    return acc


def _dot_f32_rhs(lhs_exact, x, n):
    parts = _split_bf16(x, n)
    acc = _dot(lhs_exact, parts[0])
    for p in parts[1:]:
        acc = acc + _dot(lhs_exact, p)
    return acc


def _rmsnorm(x, w):
    return x * lax.rsqrt(jnp.mean(x * x, axis=-1, keepdims=True) + EPS) * w


D_IN_PROJ = SSD_WIDTH + CONV_DIM + SSD_HEADS + 3 * SB_WIDTH + POOL_WIDTH
_Z_COLS = (0, SSD_WIDTH)
_XBC_COLS = (SSD_WIDTH, SSD_WIDTH + CONV_DIM)
_TAIL_COLS = (_XBC_COLS[1], D_IN_PROJ)


def _layer_spec(shape, layer, grid_rank, single_buffer=False):
    zeros = (0,) * len(shape)
    index_map = {1: lambda i: (layer,) + zeros, 2: lambda i, j: (layer,) + zeros}[grid_rank]
    kwargs = {"pipeline_mode": pl.Buffered(1)} if single_buffer else {}
    return pl.BlockSpec((None,) + tuple(shape), index_map, **kwargs)


def _norm_inproj_kernel(x_ref, nw_ref, w_ref, z_ref, xbc_ref, qkv_ref, p_ref, dt_ref):
    hb = _rmsnorm(x_ref[...], nw_ref[...]).astype(BF16)
    z_ref[...] = _dot(hb, w_ref[:, _Z_COLS[0]:_Z_COLS[1]])
    xbc_ref[...] = _dot(hb, w_ref[:, _XBC_COLS[0]:_XBC_COLS[1]])
    tail = _dot(hb, w_ref[:, _TAIL_COLS[0]:_TAIL_COLS[1]])
    dt_ref[...] = tail[:, 0:DT_PAD]
    qkv_ref[...] = tail[:, SSD_HEADS:SSD_HEADS + 3 * SB_WIDTH]
    p_ref[...] = tail[:, SSD_HEADS + 3 * SB_WIDTH:]


def _norm_inproj(x2, norm_w, w_in, layer):
    nt = x2.shape[0]
    row = lambda width: pl.BlockSpec((TM_PROJ, width), lambda i: (i, 0))
    widths = [SSD_WIDTH, CONV_DIM, 3 * SB_WIDTH, POOL_WIDTH, DT_PAD]
    return pl.pallas_call(
        _norm_inproj_kernel,
        grid=(nt // TM_PROJ,),
        in_specs=[row(D_MODEL), _layer_spec((1, D_MODEL), layer, 1),
                  _layer_spec((D_MODEL, D_IN_PROJ), layer, 1, single_buffer=True)],
        out_specs=[row(w) for w in widths],
        out_shape=[jax.ShapeDtypeStruct((nt, w), F32) for w in widths],
        compiler_params=pltpu.CompilerParams(
            dimension_semantics=("arbitrary",), vmem_limit_bytes=VMEM_LIMIT),
        name="norm_inproj",
    )(x2, norm_w, w_in)


def _ssd_kernel(z_ref, xbc_ref, dt_ref, convw_ref, convb_ref, dtb_ref, alog_ref, dskip_ref,
                nw_ref, tri_ref, upper_ref, expand_ref, o_ref, ext_ref, state_ref):
    c = pl.program_id(1)

    @pl.when(c == 0)
    def _():
        ext_ref[:, 0:SUBLANES, :] = jnp.zeros((SSD_BATCH_TILE, SUBLANES, CONV_DIM), F32)
        state_ref[...] = jnp.zeros_like(state_ref)

    for e in range(SSD_BATCH_TILE):
        _ssd_chunk(z_ref.at[e], xbc_ref.at[e], dt_ref.at[e], convw_ref, convb_ref, dtb_ref,
                   alog_ref, dskip_ref, nw_ref, tri_ref, upper_ref, expand_ref, o_ref.at[e],
                   ext_ref.at[e], state_ref.at[e])


def _ssd_chunk(z_ref, xbc_ref, dt_ref, convw_ref, convb_ref, dtb_ref, alog_ref, dskip_ref,
               nw_ref, tri_ref, upper_ref, expand_ref, o_ref, ext_ref, state_ref):
    L = CHUNK
    ext_ref[SUBLANES:SUBLANES + L, :] = xbc_ref[...]
    full = ext_ref[...]
    u = convb_ref[...] + convw_ref[CONV_WIDTH - 1:CONV_WIDTH, :] * full[SUBLANES:, :]
    shifted = full
    for i in range(CONV_WIDTH - 2, -1, -1):
        shifted = pltpu.roll(shifted, 1, axis=0)
        u = u + convw_ref[i:i + 1, :] * shifted[SUBLANES:, :]
    ext_ref[0:SUBLANES, :] = ext_ref[L:L + SUBLANES, :]
    act = _silu(u)
    xs = act[:, 0:SSD_WIDTH]
    xs_b = xs.astype(BF16)
    bc = act[:, SSD_WIDTH:CONV_DIM].astype(BF16)

    dt = _softplus(dt_ref[...] + dtb_ref[...])
    a = -jnp.exp(alog_ref[...])
    d_a = dt * a

    row = lax.broadcasted_iota(jnp.int32, (L, L), 0)
    col = lax.broadcasted_iota(jnp.int32, (L, L), 1)
    causal = col <= row
    acum = _dot_f32_rhs(tri_ref[...], d_a, 3)
    d_a_t = jnp.transpose(d_a)[0:2 * SUBLANES, :]
    acum_t = _dot_f32_lhs(d_a_t, upper_ref[...], 3)
    dt_t = jnp.transpose(dt)[0:2 * SUBLANES, :]

    total = acum[L - 1:L, :]
    eac = jnp.exp(acum)
    f_state = dt * jnp.exp(total - acum)

    expand = expand_ref[...]
    eac_x = _dot_f32_lhs(eac, expand, 2)
    f_state_x = _dot_f32_lhs(f_state, expand, 2)

    lane = lax.broadcasted_iota(jnp.int32, (L, LANES), 1)
    low_half = lane < SSD_HEAD_DIM

    y_parts = []
    for g in range(SSD_GROUPS):
        b_g = bc[:, g * D_STATE:(g + 1) * D_STATE]
        c_off = SSD_GROUPS * D_STATE
        c_g = bc[:, c_off + g * D_STATE:c_off + (g + 1) * D_STATE]
        cb = lax.dot_general(c_g, b_g, (((1,), (1,)), ((), ())), preferred_element_type=F32)
        gw = SSD_WIDTH // SSD_GROUPS
        st = state_ref[g]
        y_off = _dot(c_g, st.astype(BF16)) * eac_x[:, g * gw:(g + 1) * gw]
        for j in range(2):
            lo = g * gw + j * LANES
            x_pair = xs_b[:, lo:lo + LANES]
            y_pair = None
            for k in range(2):
                h = (lo // SSD_HEAD_DIM) + k
                seg = acum[:, h:h + 1] - acum_t[h:h + 1, :]
                decay = jnp.exp(jnp.where(causal, seg, -jnp.inf))
                m = (cb * decay * dt_t[h:h + 1, :]).astype(BF16)
                x_h = jnp.where(low_half if k == 0 else jnp.logical_not(low_half), x_pair, 0)
                contrib = _dot(m, x_h)
                y_pair = contrib if y_pair is None else y_pair + contrib
            y_parts.append(y_pair + y_off[:, j * LANES:(j + 1) * LANES])
        xd = (xs[:, g * gw:(g + 1) * gw] * f_state_x[:, g * gw:(g + 1) * gw]).astype(BF16)
        s_add = lax.dot_general(b_g, xd, (((0,), (0,)), ((), ())), preferred_element_type=F32)
        state_ref[g] = st * eac_x[L - 1:L, g * gw:(g + 1) * gw] + s_add

    y = jnp.concatenate(y_parts, axis=1) + xs * dskip_ref[...]
    y = y * _silu(z_ref[...])
    o_ref[...] = _rmsnorm(y, nw_ref[...])


def _ssd(z, xbc, dt, conv_w, conv_b, dt_bias, a_log, d_skip_x, norm_w, consts, layer, batch,
         seq):
    nc = seq // CHUNK
    as_seq = lambda t: t.reshape(batch, seq, t.shape[-1])
    row = lambda width: pl.BlockSpec((SSD_BATCH_TILE, CHUNK, width), lambda b, c: (b, c, 0))
    const = lambda shape: pl.BlockSpec(shape, lambda b, c: (0, 0))
    spec = lambda shape: _layer_spec(shape, layer, 2)
    tri, upper, expand = consts
    return pl.pallas_call(
        _ssd_kernel,
        grid=(batch // SSD_BATCH_TILE, nc),
        in_specs=[row(SSD_WIDTH), row(CONV_DIM), row(DT_PAD),
                  spec((CONV_WIDTH, CONV_DIM)), spec((1, CONV_DIM)), spec((1, DT_PAD)),
                  spec((1, DT_PAD)), spec((1, SSD_WIDTH)), spec((1, SSD_WIDTH)),
                  const((CHUNK, CHUNK)), const((CHUNK, CHUNK)), const((DT_PAD, SSD_WIDTH))],
        out_specs=row(SSD_WIDTH),
        out_shape=jax.ShapeDtypeStruct((batch, seq, SSD_WIDTH), F32),
        scratch_shapes=[pltpu.VMEM((SSD_BATCH_TILE, CHUNK + SUBLANES, CONV_DIM), F32),
                        pltpu.VMEM((SSD_BATCH_TILE, SSD_GROUPS, D_STATE,
                                    SSD_WIDTH // SSD_GROUPS), F32)],
        compiler_params=pltpu.CompilerParams(
            dimension_semantics=("arbitrary", "arbitrary"), vmem_limit_bytes=VMEM_LIMIT),
        name="ssd_mixer",
    )(as_seq(z), as_seq(xbc), as_seq(dt), conv_w, conv_b, dt_bias, a_log, d_skip_x, norm_w, tri,
      upper, expand).reshape(batch * seq, SSD_WIDTH)


def _ssd_constants():
    idx = jnp.arange(CHUNK)
    tri = (idx[None, :] <= idx[:, None]).astype(BF16)
    upper = (idx[:, None] <= idx[None, :]).astype(BF16)
    expand = (jnp.arange(SSD_WIDTH)[None, :] // SSD_HEAD_DIM
              == jnp.arange(DT_PAD)[:, None]).astype(BF16)
    return tri, upper, expand


def _sb_kernel(q_ref, k_ref, v_ref, o_ref, kb_ref, vbm_ref, qs_ref, acc_ref, carry_ref):
    T = SB_BLOCK
    H = SB_HEADS
    U = 2 * H
    si = pl.program_id(1)
    head_of_lane = jnp.right_shift(lax.broadcasted_iota(jnp.int32, (1, SB_WIDTH), 1),
                                   HEAD_DIM_SHIFT)

    @pl.when(si == 0)
    def _():
        kb_ref[...] = k_ref[...].astype(BF16)
        v = v_ref[...]
        for h in range(H):
            vbm_ref[h] = jnp.where(head_of_lane == h, v, 0.0).astype(BF16)

    for u in range(U):
        q = q_ref[(u // H) * T:(u // H + 1) * T, :] * (SB_HEAD_DIM ** -0.5)
        qs_ref[u * T:(u + 1) * T, :] = jnp.where(head_of_lane == u % H, q, 0.0).astype(BF16)

    row = lax.broadcasted_iota(jnp.int32, (T, T), 0)
    col = lax.broadcasted_iota(jnp.int32, (T, T), 1)
    before = col < row
    tri = (row > col).astype(BF16)

    def scores(u0, n_units, j):
        kb = kb_ref[pl.ds(pl.multiple_of(j * T, T), T), :]
        return lax.dot_general(qs_ref[u0 * T:(u0 + n_units) * T, :], kb,
                               (((1,), (1,)), ((), ())), preferred_element_type=F32)

    def weights(zs, carries, diagonal):
        ws = []
        new_carries = []
        for i in range(zs.shape[0] // T):
            z = zs[i * T:(i + 1) * T, :]
            e = jnp.exp2(jnp.abs(z) * (-LOG2E))
            sp = jnp.maximum(z, 0.0) + jnp.log(1.0 + e)
            if diagonal:
                sp = jnp.where(before, sp, 0.0)
            arg = z - sp - _dot(sp.astype(BF16), tri)
            if carries is not None:
                arg = arg - jnp.sum(carries[i], axis=1, keepdims=True)
            w = jnp.exp2(arg * LOG2E)
            if diagonal:
                w = jnp.where(before, w, 0.0)
            ws.append(w.astype(BF16))
            part = sp[:, 0:LANES] + sp[:, LANES:2 * LANES]
            new_carries.append(part if carries is None else carries[i] + part)
        return ws, new_carries

    def values(ws, j):
        start = pl.multiple_of(j * T, T)
        n_q = len(ws) // H
        pv = None
        for h in range(H):
            lhs = ws[h] if n_q == 1 else jnp.concatenate([ws[h], ws[H + h]], axis=0)
            t = _dot(lhs, vbm_ref[h, pl.ds(start, T), :])
            pv = t if pv is None else pv + t
        return pv

    ja = 2 * si
    ws_a, carries_a = weights(scores(0, H, ja), None, True)
    ws_b1, carries_b = weights(scores(H, H, ja + 1), None, True)
    ws_b0, carries_b = weights(scores(H, H, ja), carries_b, False)
    acc_ref[0:T, :] = values(ws_a, ja)
    acc_ref[T:2 * T, :] = values(ws_b1, ja + 1) + values(ws_b0, ja)
    for u, c in enumerate(carries_a + carries_b):
        carry_ref[u] = c

    def body(it, _):
        j = ja - 1 - 2 * it
        ws1, carries = weights(scores(0, U, j), [carry_ref[u] for u in range(U)], False)
        ws0, carries = weights(scores(0, U, j - 1), carries, False)
        acc_ref[...] += values(ws1, j) + values(ws0, j - 1)
        for u in range(U):
            carry_ref[u] = carries[u]
        return 0

    lax.fori_loop(0, si, body, 0)
    o_ref[...] = acc_ref[...]


def _sb_attention(qkv, batch, seq):
    nt = qkv.shape[0]
    rows = 2 * SB_BLOCK
    ns = seq // rows
    return pl.pallas_call(
        _sb_kernel,
        grid=(batch, ns),
        in_specs=[pl.BlockSpec((rows, SB_WIDTH), lambda b, i: (b * ns + i, 0)),
                  pl.BlockSpec((seq, SB_WIDTH), lambda b, i: (b, 1)),
                  pl.BlockSpec((seq, SB_WIDTH), lambda b, i: (b, 2))],
        out_specs=pl.BlockSpec((rows, SB_WIDTH), lambda b, i: (b * ns + i, 0)),
        out_shape=jax.ShapeDtypeStruct((nt, SB_WIDTH), F32),
        scratch_shapes=[pltpu.VMEM((seq, SB_WIDTH), BF16),
                        pltpu.VMEM((SB_HEADS, seq, SB_WIDTH), BF16),
                        pltpu.VMEM((2 * SB_HEADS * SB_BLOCK, SB_WIDTH), BF16),
                        pltpu.VMEM((rows, SB_WIDTH), F32),
                        pltpu.VMEM((2 * SB_HEADS, SB_BLOCK, LANES), F32)],
        compiler_params=pltpu.CompilerParams(
            dimension_semantics=("arbitrary", "arbitrary"), vmem_limit_bytes=VMEM_LIMIT),
        name="sb_attention",
    )(qkv, qkv, qkv)


def _pool_out_ffn_kernel(x_ref, yssd_ref, ysb_ref, p_ref, halo_ref, poolw_ref, poolb_ref,
                         pscale_ref, wout_ref, n2_ref, wg_ref, wu_ref, wd_ref, fn_ref,
                         o_ref, buf_a, buf_b, *, tiles_per_seq, final_norm):
    TM = TM_FFN
    i = pl.program_id(0)
    t0 = (i % tiles_per_seq) * TM

    top = SUBLANES + POOL_HALO
    p_cur = p_ref[...]
    zeros8 = jnp.zeros((SUBLANES, POOL_WIDTH), F32)
    buf_a[0:SUBLANES, :] = zeros8
    buf_b[0:SUBLANES, :] = zeros8
    buf_a[SUBLANES:top, :] = jnp.where(t0 == 0, 0.0, halo_ref[...])
    buf_a[top:top + TM, :] = p_cur
    lane = lax.broadcasted_iota(jnp.int32, (TM, POOL_WIDTH), 1)
    group = jnp.right_shift(lane, HEAD_DIM_SHIFT)
    n_rows = TM + POOL_HALO
    wsum = None
    src, dst = buf_a, buf_b
    for gi, win in enumerate(POOL_WINDOWS):
        shift = win // 2
        s = src[SUBLANES:SUBLANES + n_rows, :] + src[SUBLANES - shift:SUBLANES - shift + n_rows, :]
        dst[SUBLANES:SUBLANES + n_rows, :] = s
        tile_sum = s[POOL_HALO:, :]
        wsum = tile_sum if wsum is None else jnp.where(group >= gi, tile_sum, wsum)
        src, dst = dst, src
    pos1 = t0 + 1 + lax.broadcasted_iota(jnp.int32, (TM, POOL_WIDTH), 0)
    win_lane = jnp.left_shift(2, group)
    count = jnp.minimum(pos1, win_lane).astype(F32)
    pooled = wsum / count - p_cur
    mixed = _dot(pooled.astype(BF16), poolw_ref[...]) + poolb_ref[...]
    y_pool = mixed * pscale_ref[...]

    a0, a1 = SSD_WIDTH, SSD_WIDTH + SB_WIDTH
    o_ref[...] = (x_ref[...]
                  + _dot(yssd_ref[...].astype(BF16), wout_ref[0:a0, :])
                  + _dot(ysb_ref[...].astype(BF16), wout_ref[a0:a1, :])
                  + _dot(y_pool.astype(BF16), wout_ref[a1:, :]))

    x1 = o_ref[...]
    hb = _rmsnorm(x1, n2_ref[...]).astype(BF16)
    acc = x1
    for lo, hi in ((0, FF_SPLIT), (FF_SPLIT, D_FF)):
        gate = _dot(hb, wg_ref[:, lo:hi])
        up = _dot(hb, wu_ref[:, lo:hi])
        act = (_silu(gate) * up).astype(BF16)
        acc = acc + _dot(act, wd_ref[lo:hi, :])
    if final_norm:
        acc = _rmsnorm(acc, fn_ref[...])
    o_ref[...] = acc


def _pool_out_ffn(x2, y_ssd, y_sb, p, pool_w_bd, pool_b, pool_scale, w_out, norm2_w, w_gate,
                  w_up, w_down, final_w, layer, seq, final_norm):
    nt = x2.shape[0]
    tiles_per_seq = seq // TM_FFN
    halo_per_tile = TM_FFN // POOL_HALO
    row = lambda width: pl.BlockSpec((TM_FFN, width), lambda i: (i, 0))
    spec = lambda shape: _layer_spec(shape, layer, 1, single_buffer=True)
    halo = pl.BlockSpec((POOL_HALO, POOL_WIDTH),
                        lambda i: (jnp.maximum(i * halo_per_tile - 1, 0), 0))
    kern = functools.partial(_pool_out_ffn_kernel, tiles_per_seq=tiles_per_seq,
                             final_norm=final_norm)
    buf_rows = SUBLANES + POOL_HALO + TM_FFN
    return pl.pallas_call(
        kern,
        grid=(nt // TM_FFN,),
        in_specs=[row(D_MODEL), row(SSD_WIDTH), row(SB_WIDTH), row(POOL_WIDTH), halo,
                  spec((POOL_WIDTH, POOL_WIDTH)), spec((1, POOL_WIDTH)), spec((1, POOL_WIDTH)),
                  spec((D_MODEL, D_MODEL)), spec((1, D_MODEL)),
                  spec((D_MODEL, D_FF)), spec((D_MODEL, D_FF)), spec((D_FF, D_MODEL)),
                  pl.BlockSpec((1, D_MODEL), lambda i: (0, 0))],
        out_specs=row(D_MODEL),
        out_shape=jax.ShapeDtypeStruct((nt, D_MODEL), F32),
        scratch_shapes=[pltpu.VMEM((buf_rows, POOL_WIDTH), F32),
                        pltpu.VMEM((buf_rows, POOL_WIDTH), F32)],
        compiler_params=pltpu.CompilerParams(
            dimension_semantics=("arbitrary",), vmem_limit_bytes=VMEM_LIMIT),
        name="pool_out_ffn",
    )(x2, y_ssd, y_sb, p, p, pool_w_bd, pool_b, pool_scale, w_out, norm2_w, w_gate, w_up,
      w_down, final_w)


def _block_diag(pool_w):
    depth, g, c, d = pool_w.shape
    eye = jnp.eye(g, dtype=pool_w.dtype)
    return (pool_w[:, :, :, None, :] * eye[None, :, None, :, None]).reshape(depth, g * c, g * d)


def _rows(v, width=None):
    if width is not None and width > v.shape[-1]:
        v = jnp.pad(v, ((0, 0), (0, width - v.shape[-1])))
    return v[:, None, :]


def kernel(x, norm1_w, w_in, conv_w, conv_b, dt_bias, a_log, d_skip, ssd_norm_w, pool_w, pool_b,
           pool_scale, w_out, norm2_w, w_gate, w_up, w_down, final_norm_w):
    batch, seq, d_model = x.shape
    x2 = x.reshape(batch * seq, d_model)
    w_in = w_in.astype(BF16)
    w_out, w_gate, w_up, w_down = (w.astype(BF16) for w in (w_out, w_gate, w_up, w_down))
    pool_w_bd = _block_diag(pool_w).astype(BF16)
    norm1_r, norm2_r, conv_b_r = _rows(norm1_w), _rows(norm2_w), _rows(conv_b)
    dt_bias_r, a_log_r = _rows(dt_bias, DT_PAD), _rows(a_log, DT_PAD)
    d_skip_r = _rows(jnp.repeat(d_skip, SSD_HEAD_DIM, axis=1))
    ssd_norm_r = _rows(ssd_norm_w)
    pool_b_r = _rows(pool_b.reshape(DEPTH, POOL_WIDTH))
    pool_scale_r = _rows(pool_scale)
    final_w = final_norm_w.reshape(1, d_model)
    ssd_consts = _ssd_constants()
    for layer in range(DEPTH):
        z, xbc, qkv, p, dt = _norm_inproj(x2, norm1_r, w_in, layer)
        y_ssd = _ssd(z, xbc, dt, conv_w, conv_b_r, dt_bias_r, a_log_r, d_skip_r, ssd_norm_r,
                     ssd_consts, layer, batch, seq)
        y_sb = _sb_attention(qkv, batch, seq)
        x2 = _pool_out_ffn(x2, y_ssd, y_sb, p, pool_w_bd, pool_b_r, pool_scale_r, w_out, norm2_r,
                           w_gate, w_up, w_down, final_w, layer, seq,
                           final_norm=(layer == DEPTH - 1))
    return x2.reshape(batch, seq, d_model)
```

```python
import functools

import jax
import jax.numpy as jnp
from jax import lax
from jax.experimental import pallas as pl
from jax.experimental.pallas import tpu as pltpu

F32 = jnp.float32
BF16 = jnp.bfloat16

D_MODEL = 1024
DEPTH = 4
SSD_WIDTH = 512
SSD_HEADS = 8
SSD_HEAD_DIM = 64
SSD_GROUPS = 2
D_STATE = 128
CONV_WIDTH = 4
CHUNK = 128
CONV_DIM = SSD_WIDTH + 2 * SSD_GROUPS * D_STATE
SB_WIDTH = 256
SB_HEADS = 4
SB_HEAD_DIM = 64
POOL_WINDOWS = (2, 4, 8, 16)
POOL_WIDTH = 256
POOL_GROUP_DIM = 64
D_FF = 2816
EPS = 1e-6
HEAD_DIM_SHIFT = 6
LOG2E = 1.4426950408889634

LANES = 128
SUBLANES = 8
DT_PAD = LANES
VMEM_LIMIT = 56 * 1024 * 1024

TM_PROJ = 1024
TM_FFN = 512
FF_EDGES = (0, 1536, 2816)
SB_BLOCK = 256
SSD_BATCH_TILE = 8
POOL_HALO = 16


def _dot(a, b):
    return jnp.dot(a, b, preferred_element_type=F32)


def _softplus(x):
    return jnp.maximum(x, 0.0) + jnp.log1p(jnp.exp(-jnp.abs(x)))


def _silu(x):
    h = 0.5 * x
    return h + h * jnp.tanh(h)


def _split_bf16(x, n):
    parts = []
    r = x
    for i in range(n):
        p = r.astype(BF16)
        parts.append(p)
        if i + 1 < n:
            r = r - p.astype(F32)
    return parts


def _dot_f32_lhs(x, rhs_exact, n):
    parts = _split_bf16(x, n)
    acc = _dot(parts[0], rhs_exact)
    for p in parts[1:]:
        acc = acc + _dot(p, rhs_exact)
    return acc


def _dot_f32_rhs(lhs_exact, x, n):
    parts = _split_bf16(x, n)
    acc = _dot(lhs_exact, parts[0])
    for p in parts[1:]:
        acc = acc + _dot(lhs_exact, p)
    return acc


def _rmsnorm(x, w):
    return x * lax.rsqrt(jnp.mean(x * x, axis=-1, keepdims=True) + EPS) * w


D_IN_PROJ = SSD_WIDTH + CONV_DIM + SSD_HEADS + 3 * SB_WIDTH + POOL_WIDTH
_Z_COLS = (0, SSD_WIDTH)
_XBC_COLS = (SSD_WIDTH, SSD_WIDTH + CONV_DIM)
_TAIL_COLS = (_XBC_COLS[1], D_IN_PROJ)


def _layer_spec(shape, layer, grid_rank, single_buffer=False):
    zeros = (0,) * len(shape)
    index_map = {1: lambda i: (layer,) + zeros, 2: lambda i, j: (layer,) + zeros}[grid_rank]
    kwargs = {"pipeline_mode": pl.Buffered(1)} if single_buffer else {}
    return pl.BlockSpec((None,) + tuple(shape), index_map, **kwargs)


def _norm_inproj_kernel(x_ref, nw_ref, wf_ref, z_ref, xbc_ref, qkv_ref, p_ref, dt_ref, w_ref):
    @pl.when(pl.program_id(0) == 0)
    def _():
        w_ref[...] = wf_ref[...].astype(BF16)

    hb = _rmsnorm(x_ref[...], nw_ref[...]).astype(BF16)
    z_ref[...] = _dot(hb, w_ref[:, _Z_COLS[0]:_Z_COLS[1]])
    xbc_ref[...] = _dot(hb, w_ref[:, _XBC_COLS[0]:_XBC_COLS[1]])
    tail = _dot(hb, w_ref[:, _TAIL_COLS[0]:_TAIL_COLS[1]])
    dt_ref[...] = tail[:, 0:DT_PAD]
    qkv_ref[...] = tail[:, SSD_HEADS:SSD_HEADS + 3 * SB_WIDTH]
    p_ref[...] = tail[:, SSD_HEADS + 3 * SB_WIDTH:]


def _norm_inproj(x2, norm_w, w_in, layer):
    nt = x2.shape[0]
    row = lambda width: pl.BlockSpec((TM_PROJ, width), lambda i: (i, 0))
    widths = [SSD_WIDTH, CONV_DIM, 3 * SB_WIDTH, POOL_WIDTH, DT_PAD]
    return pl.pallas_call(
        _norm_inproj_kernel,
        grid=(nt // TM_PROJ,),
        in_specs=[row(D_MODEL), _layer_spec((1, D_MODEL), layer, 1),
                  _layer_spec((D_MODEL, D_IN_PROJ), layer, 1, single_buffer=True)],
        out_specs=[row(w) for w in widths],
        out_shape=[jax.ShapeDtypeStruct((nt, w), F32) for w in widths],
        scratch_shapes=[pltpu.VMEM((D_MODEL, D_IN_PROJ), BF16)],
        compiler_params=pltpu.CompilerParams(
            dimension_semantics=("arbitrary",), vmem_limit_bytes=VMEM_LIMIT),
        name="norm_inproj",
    )(x2, norm_w, w_in)


def _ssd_kernel(z_ref, xbc_ref, dt_ref, convw_ref, convb_ref, dtb_ref, alog_ref, dskip_ref,
                nw_ref, tri_ref, upper_ref, expand_ref, o_ref, ext_ref, state_ref):
    c = pl.program_id(1)

    @pl.when(c == 0)
    def _():
        ext_ref[:, 0:SUBLANES, :] = jnp.zeros((SSD_BATCH_TILE, SUBLANES, CONV_DIM), F32)
        state_ref[...] = jnp.zeros_like(state_ref)

    for e in range(SSD_BATCH_TILE):
        _ssd_chunk(z_ref.at[e], xbc_ref.at[e], dt_ref.at[e], convw_ref, convb_ref, dtb_ref,
                   alog_ref, dskip_ref, nw_ref, tri_ref, upper_ref, expand_ref, o_ref.at[e],
                   ext_ref.at[e], state_ref.at[e])


def _ssd_chunk(z_ref, xbc_ref, dt_ref, convw_ref, convb_ref, dtb_ref, alog_ref, dskip_ref,
               nw_ref, tri_ref, upper_ref, expand_ref, o_ref, ext_ref, state_ref):
    L = CHUNK
    ext_ref[SUBLANES:SUBLANES + L, :] = xbc_ref[...]
    full = ext_ref[...]
    u = convb_ref[...] + convw_ref[CONV_WIDTH - 1:CONV_WIDTH, :] * full[SUBLANES:, :]
    shifted = full
    for i in range(CONV_WIDTH - 2, -1, -1):
        shifted = pltpu.roll(shifted, 1, axis=0)
        u = u + convw_ref[i:i + 1, :] * shifted[SUBLANES:, :]
    ext_ref[0:SUBLANES, :] = ext_ref[L:L + SUBLANES, :]
    act = _silu(u)
    xs = act[:, 0:SSD_WIDTH]
    xs_b = xs.astype(BF16)
    bc = act[:, SSD_WIDTH:CONV_DIM].astype(BF16)

    dt = _softplus(dt_ref[...] + dtb_ref[...])
    a = -jnp.exp(alog_ref[...])
    d_a = dt * a

    row = lax.broadcasted_iota(jnp.int32, (L, L), 0)
    col = lax.broadcasted_iota(jnp.int32, (L, L), 1)
    causal = col <= row
    acum = _dot_f32_rhs(tri_ref[...], d_a, 3)
    d_a_t = jnp.transpose(d_a)[0:2 * SUBLANES, :]
    acum_t = _dot_f32_lhs(d_a_t, upper_ref[...], 3)
    dt_t = jnp.transpose(dt)[0:2 * SUBLANES, :]

    total = acum[L - 1:L, :]
    eac = jnp.exp(acum)
    f_state = dt * jnp.exp(total - acum)

    expand = expand_ref[...]
    eac_x = _dot_f32_lhs(eac, expand, 2)
    f_state_x = _dot_f32_lhs(f_state, expand, 2)

    lane = lax.broadcasted_iota(jnp.int32, (L, LANES), 1)
    low_half = lane < SSD_HEAD_DIM

    y_parts = []
    for g in range(SSD_GROUPS):
        b_g = bc[:, g * D_STATE:(g + 1) * D_STATE]
        c_off = SSD_GROUPS * D_STATE
        c_g = bc[:, c_off + g * D_STATE:c_off + (g + 1) * D_STATE]
        cb = lax.dot_general(c_g, b_g, (((1,), (1,)), ((), ())), preferred_element_type=F32)
        gw = SSD_WIDTH // SSD_GROUPS
        st = state_ref[g]
        y_off = _dot(c_g, st.astype(BF16)) * eac_x[:, g * gw:(g + 1) * gw]
        for j in range(2):
            lo = g * gw + j * LANES
            x_pair = xs_b[:, lo:lo + LANES]
            y_pair = None
            for k in range(2):
                h = (lo // SSD_HEAD_DIM) + k
                seg = acum[:, h:h + 1] - acum_t[h:h + 1, :]
                decay = jnp.exp(jnp.where(causal, seg, -jnp.inf))
                m = (cb * decay * dt_t[h:h + 1, :]).astype(BF16)
                x_h = jnp.where(low_half if k == 0 else jnp.logical_not(low_half), x_pair, 0)
                contrib = _dot(m, x_h)
                y_pair = contrib if y_pair is None else y_pair + contrib
            y_parts.append(y_pair + y_off[:, j * LANES:(j + 1) * LANES])
        xd = (xs[:, g * gw:(g + 1) * gw] * f_state_x[:, g * gw:(g + 1) * gw]).astype(BF16)
        s_add = lax.dot_general(b_g, xd, (((0,), (0,)), ((), ())), preferred_element_type=F32)
        state_ref[g] = st * eac_x[L - 1:L, g * gw:(g + 1) * gw] + s_add

    y = jnp.concatenate(y_parts, axis=1) + xs * dskip_ref[...]
    y = y * _silu(z_ref[...])
    o_ref[...] = _rmsnorm(y, nw_ref[...])


def _ssd(z, xbc, dt, conv_w, conv_b, dt_bias, a_log, d_skip_x, norm_w, consts, layer, batch,
         seq):
    nc = seq // CHUNK
    as_seq = lambda t: t.reshape(batch, seq, t.shape[-1])
    row = lambda width: pl.BlockSpec((SSD_BATCH_TILE, CHUNK, width), lambda b, c: (b, c, 0))
    const = lambda shape: pl.BlockSpec(shape, lambda b, c: (0, 0))
    spec = lambda shape: _layer_spec(shape, layer, 2)
    tri, upper, expand = consts
    return pl.pallas_call(
        _ssd_kernel,
        grid=(batch // SSD_BATCH_TILE, nc),
        in_specs=[row(SSD_WIDTH), row(CONV_DIM), row(DT_PAD),
                  spec((CONV_WIDTH, CONV_DIM)), spec((1, CONV_DIM)), spec((1, DT_PAD)),
                  spec((1, DT_PAD)), spec((1, SSD_WIDTH)), spec((1, SSD_WIDTH)),
                  const((CHUNK, CHUNK)), const((CHUNK, CHUNK)), const((DT_PAD, SSD_WIDTH))],
        out_specs=row(SSD_WIDTH),
        out_shape=jax.ShapeDtypeStruct((batch, seq, SSD_WIDTH), F32),
        scratch_shapes=[pltpu.VMEM((SSD_BATCH_TILE, CHUNK + SUBLANES, CONV_DIM), F32),
                        pltpu.VMEM((SSD_BATCH_TILE, SSD_GROUPS, D_STATE,
                                    SSD_WIDTH // SSD_GROUPS), F32)],
        compiler_params=pltpu.CompilerParams(
            dimension_semantics=("arbitrary", "arbitrary"), vmem_limit_bytes=VMEM_LIMIT),
        name="ssd_mixer",
    )(as_seq(z), as_seq(xbc), as_seq(dt), conv_w, conv_b, dt_bias, a_log, d_skip_x, norm_w, tri,
      upper, expand).reshape(batch * seq, SSD_WIDTH)


def _ssd_constants():
    idx = jnp.arange(CHUNK)
    tri = (idx[None, :] <= idx[:, None]).astype(BF16)
    upper = (idx[:, None] <= idx[None, :]).astype(BF16)
    expand = (jnp.arange(SSD_WIDTH)[None, :] // SSD_HEAD_DIM
              == jnp.arange(DT_PAD)[:, None]).astype(BF16)
    return tri, upper, expand


def _sb_kernel(q_ref, k_ref, v_ref, o_ref, kb_ref, vbm_ref, qs_ref, acc_ref, carry_ref):
    T = SB_BLOCK
    H = SB_HEADS
    U = 2 * H
    si = pl.program_id(1)
    head_of_lane = jnp.right_shift(lax.broadcasted_iota(jnp.int32, (1, SB_WIDTH), 1),
                                   HEAD_DIM_SHIFT)

    @pl.when(si == 0)
    def _():
        kb_ref[...] = k_ref[...].astype(BF16)
        v = v_ref[...]
        for h in range(H):
            vbm_ref[h] = jnp.where(head_of_lane == h, v, 0.0).astype(BF16)

    for u in range(U):
        q = q_ref[(u // H) * T:(u // H + 1) * T, :] * (SB_HEAD_DIM ** -0.5)
        qs_ref[u * T:(u + 1) * T, :] = jnp.where(head_of_lane == u % H, q, 0.0).astype(BF16)

    row = lax.broadcasted_iota(jnp.int32, (T, T), 0)
    col = lax.broadcasted_iota(jnp.int32, (T, T), 1)
    before = col < row
    tri = (row > col).astype(BF16)

    def scores(u0, n_units, j):
        kb = kb_ref[pl.ds(pl.multiple_of(j * T, T), T), :]
        return lax.dot_general(qs_ref[u0 * T:(u0 + n_units) * T, :], kb,
                               (((1,), (1,)), ((), ())), preferred_element_type=F32)

    def weights(zs, carries, diagonal):
        ws = []
        new_carries = []
        for i in range(zs.shape[0] // T):
            z = zs[i * T:(i + 1) * T, :]
            e = jnp.exp2(jnp.abs(z) * (-LOG2E))
            sp = jnp.maximum(z, 0.0) + jnp.log(1.0 + e)
            if diagonal:
                sp = jnp.where(before, sp, 0.0)
            arg = z - sp - _dot(sp.astype(BF16), tri)
            if carries is not None:
                arg = arg - jnp.sum(carries[i], axis=1, keepdims=True)
            w = jnp.exp2(arg * LOG2E)
            if diagonal:
                w = jnp.where(before, w, 0.0)
            ws.append(w.astype(BF16))
            part = sp[:, 0:LANES] + sp[:, LANES:2 * LANES]
            new_carries.append(part if carries is None else carries[i] + part)
        return ws, new_carries

    def values(ws, j):
        start = pl.multiple_of(j * T, T)
        n_q = len(ws) // H
        pv = None
        for h in range(H):
            lhs = ws[h] if n_q == 1 else jnp.concatenate([ws[h], ws[H + h]], axis=0)
            t = _dot(lhs, vbm_ref[h, pl.ds(start, T), :])
            pv = t if pv is None else pv + t
        return pv

    ja = 2 * si
    ws_a, carries_a = weights(scores(0, H, ja), None, True)
    ws_b1, carries_b = weights(scores(H, H, ja + 1), None, True)
    ws_b0, carries_b = weights(scores(H, H, ja), carries_b, False)
    acc_ref[0:T, :] = values(ws_a, ja)
    acc_ref[T:2 * T, :] = values(ws_b1, ja + 1) + values(ws_b0, ja)
    for u, c in enumerate(carries_a + carries_b):
        carry_ref[u] = c

    def body(it, _):
        j = ja - 1 - 2 * it
        ws1, carries = weights(scores(0, U, j), [carry_ref[u] for u in range(U)], False)
        ws0, carries = weights(scores(0, U, j - 1), carries, False)
        acc_ref[...] += values(ws1, j) + values(ws0, j - 1)
        for u in range(U):
            carry_ref[u] = carries[u]
        return 0

    lax.fori_loop(0, si, body, 0)
    o_ref[...] = acc_ref[...]


def _sb_attention(qkv, batch, seq):
    nt = qkv.shape[0]
    rows = 2 * SB_BLOCK
    ns = seq // rows
    return pl.pallas_call(
        _sb_kernel,
        grid=(batch, ns),
        in_specs=[pl.BlockSpec((rows, SB_WIDTH), lambda b, i: (b * ns + i, 0)),
                  pl.BlockSpec((seq, SB_WIDTH), lambda b, i: (b, 1)),
                  pl.BlockSpec((seq, SB_WIDTH), lambda b, i: (b, 2))],
        out_specs=pl.BlockSpec((rows, SB_WIDTH), lambda b, i: (b * ns + i, 0)),
        out_shape=jax.ShapeDtypeStruct((nt, SB_WIDTH), F32),
        scratch_shapes=[pltpu.VMEM((seq, SB_WIDTH), BF16),
                        pltpu.VMEM((SB_HEADS, seq, SB_WIDTH), BF16),
                        pltpu.VMEM((2 * SB_HEADS * SB_BLOCK, SB_WIDTH), BF16),
                        pltpu.VMEM((rows, SB_WIDTH), F32),
                        pltpu.VMEM((2 * SB_HEADS, SB_BLOCK, LANES), F32)],
        compiler_params=pltpu.CompilerParams(
            dimension_semantics=("arbitrary", "arbitrary"), vmem_limit_bytes=VMEM_LIMIT),
        name="sb_attention",
    )(qkv, qkv, qkv)


def _pool_out_ffn_kernel(x_ref, yssd_ref, ysb_ref, p_ref, halo_ref, poolw_ref, poolb_ref,
                         pscale_ref, woutf_ref, n2_ref, wg_ref, wu_ref, wd_ref, fn_ref,
                         o_ref, buf_a, buf_b, wout_ref, *, tiles_per_seq, final_norm):
    TM = TM_FFN
    i = pl.program_id(0)
    t0 = (i % tiles_per_seq) * TM

    @pl.when(i == 0)
    def _():
        wout_ref[...] = woutf_ref[...].astype(BF16)

    top = SUBLANES + POOL_HALO
    p_cur = p_ref[...]
    zeros8 = jnp.zeros((SUBLANES, POOL_WIDTH), F32)
    buf_a[0:SUBLANES, :] = zeros8
    buf_b[0:SUBLANES, :] = zeros8
    buf_a[SUBLANES:top, :] = jnp.where(t0 == 0, 0.0, halo_ref[...])
    buf_a[top:top + TM, :] = p_cur
    lane = lax.broadcasted_iota(jnp.int32, (TM, POOL_WIDTH), 1)
    group = jnp.right_shift(lane, HEAD_DIM_SHIFT)
    n_rows = TM + POOL_HALO
    wsum = None
    src, dst = buf_a, buf_b
    for gi, win in enumerate(POOL_WINDOWS):
        shift = win // 2
        s = src[SUBLANES:SUBLANES + n_rows, :] + src[SUBLANES - shift:SUBLANES - shift + n_rows, :]
        dst[SUBLANES:SUBLANES + n_rows, :] = s
        tile_sum = s[POOL_HALO:, :]
        wsum = tile_sum if wsum is None else jnp.where(group >= gi, tile_sum, wsum)
        src, dst = dst, src
    pos1 = t0 + 1 + lax.broadcasted_iota(jnp.int32, (TM, POOL_WIDTH), 0)
    win_lane = jnp.left_shift(2, group)
    count = jnp.minimum(pos1, win_lane).astype(F32)
    pooled = wsum / count - p_cur
    mixed = _dot(pooled.astype(BF16), poolw_ref[...]) + poolb_ref[...]
    y_pool = mixed * pscale_ref[...]

    a0, a1 = SSD_WIDTH, SSD_WIDTH + SB_WIDTH
    o_ref[...] = (x_ref[...]
                  + _dot(yssd_ref[...].astype(BF16), wout_ref[0:a0, :])
                  + _dot(ysb_ref[...].astype(BF16), wout_ref[a0:a1, :])
                  + _dot(y_pool.astype(BF16), wout_ref[a1:, :]))

    x1 = o_ref[...]
    hb = _rmsnorm(x1, n2_ref[...]).astype(BF16)
    acc = x1
    for lo, hi in zip(FF_EDGES[:-1], FF_EDGES[1:]):
        gate = _dot(hb, wg_ref[:, lo:hi])
        up = _dot(hb, wu_ref[:, lo:hi])
        act = (_silu(gate) * up).astype(BF16)
        acc = acc + _dot(act, wd_ref[lo:hi, :])
    if final_norm:
        acc = _rmsnorm(acc, fn_ref[...])
    o_ref[...] = acc


def _pool_out_ffn(x2, y_ssd, y_sb, p, pool_w_bd, pool_b, pool_scale, w_out, norm2_w, w_gate,
                  w_up, w_down, final_w, layer, seq, final_norm):
    nt = x2.shape[0]
    tiles_per_seq = seq // TM_FFN
    halo_per_tile = TM_FFN // POOL_HALO
    row = lambda width: pl.BlockSpec((TM_FFN, width), lambda i: (i, 0))
    spec = lambda shape: _layer_spec(shape, layer, 1, single_buffer=True)
    halo = pl.BlockSpec((POOL_HALO, POOL_WIDTH),
                        lambda i: (jnp.maximum(i * halo_per_tile - 1, 0), 0))
    kern = functools.partial(_pool_out_ffn_kernel, tiles_per_seq=tiles_per_seq,
                             final_norm=final_norm)
    buf_rows = SUBLANES + POOL_HALO + TM_FFN
    return pl.pallas_call(
        kern,
        grid=(nt // TM_FFN,),
        in_specs=[row(D_MODEL), row(SSD_WIDTH), row(SB_WIDTH), row(POOL_WIDTH), halo,
                  spec((POOL_WIDTH, POOL_WIDTH)), spec((1, POOL_WIDTH)), spec((1, POOL_WIDTH)),
                  spec((D_MODEL, D_MODEL)), spec((1, D_MODEL)),
                  spec((D_MODEL, D_FF)), spec((D_MODEL, D_FF)), spec((D_FF, D_MODEL)),
                  pl.BlockSpec((1, D_MODEL), lambda i: (0, 0))],
        out_specs=row(D_MODEL),
        out_shape=jax.ShapeDtypeStruct((nt, D_MODEL), F32),
        scratch_shapes=[pltpu.VMEM((buf_rows, POOL_WIDTH), F32),
                        pltpu.VMEM((buf_rows, POOL_WIDTH), F32),
                        pltpu.VMEM((D_MODEL, D_MODEL), BF16)],
        compiler_params=pltpu.CompilerParams(
            dimension_semantics=("arbitrary",), vmem_limit_bytes=VMEM_LIMIT),
        name="pool_out_ffn",
    )(x2, y_ssd, y_sb, p, p, pool_w_bd, pool_b, pool_scale, w_out, norm2_w, w_gate, w_up,
      w_down, final_w)


def _block_diag(pool_w):
    depth, g, c, d = pool_w.shape
    eye = jnp.eye(g, dtype=pool_w.dtype)
    return (pool_w[:, :, :, None, :] * eye[None, :, None, :, None]).reshape(depth, g * c, g * d)


def _rows(v, width=None):
    if width is not None and width > v.shape[-1]:
        v = jnp.pad(v, ((0, 0), (0, width - v.shape[-1])))
    return v[:, None, :]


def kernel(x, norm1_w, w_in, conv_w, conv_b, dt_bias, a_log, d_skip, ssd_norm_w, pool_w, pool_b,
           pool_scale, w_out, norm2_w, w_gate, w_up, w_down, final_norm_w):
    batch, seq, d_model = x.shape
    x2 = x.reshape(batch * seq, d_model)
    w_gate, w_up, w_down = (w.astype(BF16) for w in (w_gate, w_up, w_down))
    pool_w_bd = _block_diag(pool_w).astype(BF16)
    norm1_r, norm2_r, conv_b_r = _rows(norm1_w), _rows(norm2_w), _rows(conv_b)
    dt_bias_r, a_log_r = _rows(dt_bias, DT_PAD), _rows(a_log, DT_PAD)
    d_skip_r = _rows(jnp.repeat(d_skip, SSD_HEAD_DIM, axis=1))
    ssd_norm_r = _rows(ssd_norm_w)
    pool_b_r = _rows(pool_b.reshape(DEPTH, POOL_WIDTH))
    pool_scale_r = _rows(pool_scale)
    final_w = final_norm_w.reshape(1, d_model)
    ssd_consts = _ssd_constants()
    for layer in range(DEPTH):
        z, xbc, qkv, p, dt = _norm_inproj(x2, norm1_r, w_in, layer)
        y_ssd = _ssd(z, xbc, dt, conv_w, conv_b_r, dt_bias_r, a_log_r, d_skip_r, ssd_norm_r,
                     ssd_consts, layer, batch, seq)
        y_sb = _sb_attention(qkv, batch, seq)
        x2 = _pool_out_ffn(x2, y_ssd, y_sb, p, pool_w_bd, pool_b_r, pool_scale_r, w_out, norm2_r,
                           w_gate, w_up, w_down, final_w, layer, seq,
                           final_norm=(layer == DEPTH - 1))
    return x2.reshape(batch, seq, d_model)
```

```python
import functools

import jax
import jax.numpy as jnp
from jax import lax
from jax.experimental import pallas as pl
from jax.experimental.pallas import tpu as pltpu

F32 = jnp.float32
BF16 = jnp.bfloat16

D_MODEL = 1024
DEPTH = 4
SSD_WIDTH = 512
SSD_HEADS = 8
SSD_HEAD_DIM = 64
SSD_GROUPS = 2
D_STATE = 128
CONV_WIDTH = 4
CHUNK = 128
CONV_DIM = SSD_WIDTH + 2 * SSD_GROUPS * D_STATE
SB_WIDTH = 256
SB_HEADS = 4
SB_HEAD_DIM = 64
POOL_WINDOWS = (2, 4, 8, 16)
POOL_WIDTH = 256
POOL_GROUP_DIM = 64
D_FF = 2816
EPS = 1e-6
HEAD_DIM_SHIFT = 6
LOG2E = 1.4426950408889634

LANES = 128
SUBLANES = 8
DT_PAD = LANES
VMEM_LIMIT = 56 * 1024 * 1024

TM_PROJ = 1024
TM_FFN = 512
FF_SPLIT = 1536
SB_BLOCK = 256
SSD_BATCH_TILE = 8
POOL_HALO = 16


def _dot(a, b):
    return jnp.dot(a, b, preferred_element_type=F32)


def _softplus(x):
    return jnp.maximum(x, 0.0) + jnp.log1p(jnp.exp(-jnp.abs(x)))


def _silu(x):
    h = 0.5 * x
    return h + h * jnp.tanh(h)


def _split_bf16(x, n):
    parts = []
    r = x
    for i in range(n):
        p = r.astype(BF16)
        parts.append(p)
        if i + 1 < n:
            r = r - p.astype(F32)
    return parts


def _dot_f32_lhs(x, rhs_exact, n):
    parts = _split_bf16(x, n)
    acc = _dot(parts[0], rhs_exact)
    for p in parts[1:]:
        acc = acc + _dot(p, rhs_exact)
    return acc


def _dot_f32_rhs(lhs_exact, x, n):
    parts = _split_bf16(x, n)
    acc = _dot(lhs_exact, parts[0])
    for p in parts[1:]:
        acc = acc + _dot(lhs_exact, p)
    return acc


def _rmsnorm(x, w):
    return x * lax.rsqrt(jnp.mean(x * x, axis=-1, keepdims=True) + EPS) * w


D_IN_PROJ = SSD_WIDTH + CONV_DIM + SSD_HEADS + 3 * SB_WIDTH + POOL_WIDTH
D_IN_PAD = -(-D_IN_PROJ // LANES) * LANES
_Z_COLS = (0, SSD_WIDTH)
_XBC_COLS = (SSD_WIDTH, SSD_WIDTH + CONV_DIM)
_TAIL_COLS = (_XBC_COLS[1], D_IN_PROJ)


def _layer_spec(shape, layer, grid_rank, single_buffer=False):
    zeros = (0,) * len(shape)
    index_map = {1: lambda i: (layer,) + zeros, 2: lambda i, j: (layer,) + zeros}[grid_rank]
    kwargs = {"pipeline_mode": pl.Buffered(1)} if single_buffer else {}
    return pl.BlockSpec((None,) + tuple(shape), index_map, **kwargs)


def _norm_inproj_kernel(x_ref, nw_ref, w_ref, z_ref, xbc_ref, qkv_ref, p_ref, dt_ref):
    hb = _rmsnorm(x_ref[...], nw_ref[...]).astype(BF16)
    z_ref[...] = _dot(hb, w_ref[:, _Z_COLS[0]:_Z_COLS[1]])
    xbc_ref[...] = _dot(hb, w_ref[:, _XBC_COLS[0]:_XBC_COLS[1]])
    tail = _dot(hb, w_ref[:, _TAIL_COLS[0]:_TAIL_COLS[1]])
    dt_ref[...] = tail[:, 0:DT_PAD]
    qkv_ref[...] = tail[:, SSD_HEADS:SSD_HEADS + 3 * SB_WIDTH]
    p_ref[...] = tail[:, SSD_HEADS + 3 * SB_WIDTH:]


def _norm_inproj(x2, norm_w, w_in, layer):
    nt = x2.shape[0]
    row = lambda width: pl.BlockSpec((TM_PROJ, width), lambda i: (i, 0))
    widths = [SSD_WIDTH, CONV_DIM, 3 * SB_WIDTH, POOL_WIDTH, DT_PAD]
    return pl.pallas_call(
        _norm_inproj_kernel,
        grid=(nt // TM_PROJ,),
        in_specs=[row(D_MODEL), _layer_spec((1, D_MODEL), layer, 1),
                  _layer_spec((D_MODEL, D_IN_PAD), layer, 1, single_buffer=True)],
        out_specs=[row(w) for w in widths],
        out_shape=[jax.ShapeDtypeStruct((nt, w), F32) for w in widths],
        compiler_params=pltpu.CompilerParams(
            dimension_semantics=("arbitrary",), vmem_limit_bytes=VMEM_LIMIT),
        name="norm_inproj",
    )(x2, norm_w, w_in)


def _ssd_kernel(z_ref, xbc_ref, dt_ref, convw_ref, convb_ref, dtb_ref, alog_ref, dskip_ref,
                nw_ref, tri_ref, upper_ref, expand_ref, o_ref, ext_ref, state_ref):
    c = pl.program_id(1)

    @pl.when(c == 0)
    def _():
        ext_ref[:, 0:SUBLANES, :] = jnp.zeros((SSD_BATCH_TILE, SUBLANES, CONV_DIM), F32)
        state_ref[...] = jnp.zeros_like(state_ref)

    for e in range(SSD_BATCH_TILE):
        _ssd_chunk(z_ref.at[e], xbc_ref.at[e], dt_ref.at[e], convw_ref, convb_ref, dtb_ref,
                   alog_ref, dskip_ref, nw_ref, tri_ref, upper_ref, expand_ref, o_ref.at[e],
                   ext_ref.at[e], state_ref.at[e])


def _ssd_chunk(z_ref, xbc_ref, dt_ref, convw_ref, convb_ref, dtb_ref, alog_ref, dskip_ref,
               nw_ref, tri_ref, upper_ref, expand_ref, o_ref, ext_ref, state_ref):
    L = CHUNK
    ext_ref[SUBLANES:SUBLANES + L, :] = xbc_ref[...]
    full = ext_ref[...]
    u = convb_ref[...] + convw_ref[CONV_WIDTH - 1:CONV_WIDTH, :] * full[SUBLANES:, :]
    shifted = full
    for i in range(CONV_WIDTH - 2, -1, -1):
        shifted = pltpu.roll(shifted, 1, axis=0)
        u = u + convw_ref[i:i + 1, :] * shifted[SUBLANES:, :]
    ext_ref[0:SUBLANES, :] = ext_ref[L:L + SUBLANES, :]
    act = _silu(u)
    xs = act[:, 0:SSD_WIDTH]
    xs_b = xs.astype(BF16)
    bc = act[:, SSD_WIDTH:CONV_DIM].astype(BF16)

    dt = _softplus(dt_ref[...] + dtb_ref[...])
    a = -jnp.exp(alog_ref[...])
    d_a = dt * a

    row = lax.broadcasted_iota(jnp.int32, (L, L), 0)
    col = lax.broadcasted_iota(jnp.int32, (L, L), 1)
    causal = col <= row
    acum = _dot_f32_rhs(tri_ref[...], d_a, 3)
    d_a_t = jnp.transpose(d_a)[0:2 * SUBLANES, :]
    acum_t = _dot_f32_lhs(d_a_t, upper_ref[...], 3)
    dt_t = jnp.transpose(dt)[0:2 * SUBLANES, :]

    total = acum[L - 1:L, :]
    eac = jnp.exp(acum)
    f_state = dt * jnp.exp(total - acum)

    expand = expand_ref[...]
    eac_x = _dot_f32_lhs(eac, expand, 2)
    f_state_x = _dot_f32_lhs(f_state, expand, 2)

    lane = lax.broadcasted_iota(jnp.int32, (L, LANES), 1)
    low_half = lane < SSD_HEAD_DIM

    y_parts = []
    for g in range(SSD_GROUPS):
        b_g = bc[:, g * D_STATE:(g + 1) * D_STATE]
        c_off = SSD_GROUPS * D_STATE
        c_g = bc[:, c_off + g * D_STATE:c_off + (g + 1) * D_STATE]
        cb = lax.dot_general(c_g, b_g, (((1,), (1,)), ((), ())), preferred_element_type=F32)
        gw = SSD_WIDTH // SSD_GROUPS
        st = state_ref[g]
        y_off = _dot(c_g, st.astype(BF16)) * eac_x[:, g * gw:(g + 1) * gw]
        for j in range(2):
            lo = g * gw + j * LANES
            x_pair = xs_b[:, lo:lo + LANES]
            y_pair = None
            for k in range(2):
                h = (lo // SSD_HEAD_DIM) + k
                seg = acum[:, h:h + 1] - acum_t[h:h + 1, :]
                decay = jnp.exp(jnp.where(causal, seg, -jnp.inf))
                m = (cb * decay * dt_t[h:h + 1, :]).astype(BF16)
                x_h = jnp.where(low_half if k == 0 else jnp.logical_not(low_half), x_pair, 0)
                contrib = _dot(m, x_h)
                y_pair = contrib if y_pair is None else y_pair + contrib
            y_parts.append(y_pair + y_off[:, j * LANES:(j + 1) * LANES])
        xd = (xs[:, g * gw:(g + 1) * gw] * f_state_x[:, g * gw:(g + 1) * gw]).astype(BF16)
        s_add = lax.dot_general(b_g, xd, (((0,), (0,)), ((), ())), preferred_element_type=F32)
        state_ref[g] = st * eac_x[L - 1:L, g * gw:(g + 1) * gw] + s_add

    y = jnp.concatenate(y_parts, axis=1) + xs * dskip_ref[...]
    y = y * _silu(z_ref[...])
    o_ref[...] = _rmsnorm(y, nw_ref[...])


def _ssd(z, xbc, dt, conv_w, conv_b, dt_bias, a_log, d_skip_x, norm_w, consts, layer, batch,
         seq):
    nc = seq // CHUNK
    as_seq = lambda t: t.reshape(batch, seq, t.shape[-1])
    row = lambda width: pl.BlockSpec((SSD_BATCH_TILE, CHUNK, width), lambda b, c: (b, c, 0))
    const = lambda shape: pl.BlockSpec(shape, lambda b, c: (0, 0))
    spec = lambda shape: _layer_spec(shape, layer, 2)
    tri, upper, expand = consts
    return pl.pallas_call(
        _ssd_kernel,
        grid=(batch // SSD_BATCH_TILE, nc),
        in_specs=[row(SSD_WIDTH), row(CONV_DIM), row(DT_PAD),
                  spec((CONV_WIDTH, CONV_DIM)), spec((1, CONV_DIM)), spec((1, DT_PAD)),
                  spec((1, DT_PAD)), spec((1, SSD_WIDTH)), spec((1, SSD_WIDTH)),
                  const((CHUNK, CHUNK)), const((CHUNK, CHUNK)), const((DT_PAD, SSD_WIDTH))],
        out_specs=row(SSD_WIDTH),
        out_shape=jax.ShapeDtypeStruct((batch, seq, SSD_WIDTH), F32),
        scratch_shapes=[pltpu.VMEM((SSD_BATCH_TILE, CHUNK + SUBLANES, CONV_DIM), F32),
                        pltpu.VMEM((SSD_BATCH_TILE, SSD_GROUPS, D_STATE,
                                    SSD_WIDTH // SSD_GROUPS), F32)],
        compiler_params=pltpu.CompilerParams(
            dimension_semantics=("arbitrary", "arbitrary"), vmem_limit_bytes=VMEM_LIMIT),
        name="ssd_mixer",
    )(as_seq(z), as_seq(xbc), as_seq(dt), conv_w, conv_b, dt_bias, a_log, d_skip_x, norm_w, tri,
      upper, expand).reshape(batch * seq, SSD_WIDTH)


def _ssd_constants():
    idx = jnp.arange(CHUNK)
    tri = (idx[None, :] <= idx[:, None]).astype(BF16)
    upper = (idx[:, None] <= idx[None, :]).astype(BF16)
    expand = (jnp.arange(SSD_WIDTH)[None, :] // SSD_HEAD_DIM
              == jnp.arange(DT_PAD)[:, None]).astype(BF16)
    return tri, upper, expand


def _sb_kernel(q_ref, k_ref, v_ref, o_ref, kb_ref, vbm_ref, qs_ref, acc_ref, carry_ref):
    T = SB_BLOCK
    H = SB_HEADS
    U = 2 * H
    si = pl.program_id(1)
    head_of_lane = jnp.right_shift(lax.broadcasted_iota(jnp.int32, (1, SB_WIDTH), 1),
                                   HEAD_DIM_SHIFT)

    @pl.when(si == 0)
    def _():
        kb_ref[...] = k_ref[...].astype(BF16)
        v = v_ref[...]
        for h in range(H):
            vbm_ref[h] = jnp.where(head_of_lane == h, v, 0.0).astype(BF16)

    for u in range(U):
        q = q_ref[(u // H) * T:(u // H + 1) * T, :] * (SB_HEAD_DIM ** -0.5)
        qs_ref[u * T:(u + 1) * T, :] = jnp.where(head_of_lane == u % H, q, 0.0).astype(BF16)

    row = lax.broadcasted_iota(jnp.int32, (T, T), 0)
    col = lax.broadcasted_iota(jnp.int32, (T, T), 1)
    before = col < row
    tri = (row > col).astype(BF16)

    def scores(u0, n_units, j):
        kb = kb_ref[pl.ds(pl.multiple_of(j * T, T), T), :]
        return lax.dot_general(qs_ref[u0 * T:(u0 + n_units) * T, :], kb,
                               (((1,), (1,)), ((), ())), preferred_element_type=F32)

    def weights(zs, carries, diagonal):
        ws = []
        new_carries = []
        for i in range(zs.shape[0] // T):
            z = zs[i * T:(i + 1) * T, :]
            e = jnp.exp2(jnp.abs(z) * (-LOG2E))
            sp = jnp.maximum(z, 0.0) + jnp.log(1.0 + e)
            if diagonal:
                sp = jnp.where(before, sp, 0.0)
            arg = z - sp - _dot(sp.astype(BF16), tri)
            if carries is not None:
                arg = arg - jnp.sum(carries[i], axis=1, keepdims=True)
            w = jnp.exp2(arg * LOG2E)
            if diagonal:
                w = jnp.where(before, w, 0.0)
            ws.append(w.astype(BF16))
            part = sp[:, 0:LANES] + sp[:, LANES:2 * LANES]
            new_carries.append(part if carries is None else carries[i] + part)
        return ws, new_carries

    def values(ws, j):
        start = pl.multiple_of(j * T, T)
        n_q = len(ws) // H
        pv = None
        for h in range(H):
            lhs = ws[h] if n_q == 1 else jnp.concatenate([ws[h], ws[H + h]], axis=0)
            t = _dot(lhs, vbm_ref[h, pl.ds(start, T), :])
            pv = t if pv is None else pv + t
        return pv

    ja = 2 * si
    ws_a, carries_a = weights(scores(0, H, ja), None, True)
    ws_b1, carries_b = weights(scores(H, H, ja + 1), None, True)
    ws_b0, carries_b = weights(scores(H, H, ja), carries_b, False)
    acc_ref[0:T, :] = values(ws_a, ja)
    acc_ref[T:2 * T, :] = values(ws_b1, ja + 1) + values(ws_b0, ja)
    for u, c in enumerate(carries_a + carries_b):
        carry_ref[u] = c

    def body(it, _):
        j = ja - 1 - 2 * it
        ws1, carries = weights(scores(0, U, j), [carry_ref[u] for u in range(U)], False)
        ws0, carries = weights(scores(0, U, j - 1), carries, False)
        acc_ref[...] += values(ws1, j) + values(ws0, j - 1)
        for u in range(U):
            carry_ref[u] = carries[u]
        return 0

    lax.fori_loop(0, si, body, 0)
    o_ref[...] = acc_ref[...]


def _sb_attention(qkv, batch, seq):
    nt = qkv.shape[0]
    rows = 2 * SB_BLOCK
    ns = seq // rows
    return pl.pallas_call(
        _sb_kernel,
        grid=(batch, ns),
        in_specs=[pl.BlockSpec((rows, SB_WIDTH), lambda b, i: (b * ns + i, 0)),
                  pl.BlockSpec((seq, SB_WIDTH), lambda b, i: (b, 1)),
                  pl.BlockSpec((seq, SB_WIDTH), lambda b, i: (b, 2))],
        out_specs=pl.BlockSpec((rows, SB_WIDTH), lambda b, i: (b * ns + i, 0)),
        out_shape=jax.ShapeDtypeStruct((nt, SB_WIDTH), F32),
        scratch_shapes=[pltpu.VMEM((seq, SB_WIDTH), BF16),
                        pltpu.VMEM((SB_HEADS, seq, SB_WIDTH), BF16),
                        pltpu.VMEM((2 * SB_HEADS * SB_BLOCK, SB_WIDTH), BF16),
                        pltpu.VMEM((rows, SB_WIDTH), F32),
                        pltpu.VMEM((2 * SB_HEADS, SB_BLOCK, LANES), F32)],
        compiler_params=pltpu.CompilerParams(
            dimension_semantics=("arbitrary", "arbitrary"), vmem_limit_bytes=VMEM_LIMIT),
        name="sb_attention",
    )(qkv, qkv, qkv)


def _pool_out_ffn_kernel(x_ref, yssd_ref, ysb_ref, p_ref, halo_ref, poolw_ref, poolb_ref,
                         pscale_ref, wout_ref, n2_ref, wg_ref, wu_ref, wd_ref, fn_ref,
                         o_ref, buf_a, buf_b, *, tiles_per_seq, final_norm):
    TM = TM_FFN
    i = pl.program_id(0)
    t0 = (i % tiles_per_seq) * TM

    top = SUBLANES + POOL_HALO
    p_cur = p_ref[...]
    zeros8 = jnp.zeros((SUBLANES, POOL_WIDTH), F32)
    buf_a[0:SUBLANES, :] = zeros8
    buf_b[0:SUBLANES, :] = zeros8
    buf_a[SUBLANES:top, :] = jnp.where(t0 == 0, 0.0, halo_ref[...])
    buf_a[top:top + TM, :] = p_cur
    lane = lax.broadcasted_iota(jnp.int32, (TM, POOL_WIDTH), 1)
    group = jnp.right_shift(lane, HEAD_DIM_SHIFT)
    n_rows = TM + POOL_HALO
    wsum = None
    src, dst = buf_a, buf_b
    for gi, win in enumerate(POOL_WINDOWS):
        shift = win // 2
        s = src[SUBLANES:SUBLANES + n_rows, :] + src[SUBLANES - shift:SUBLANES - shift + n_rows, :]
        dst[SUBLANES:SUBLANES + n_rows, :] = s
        tile_sum = s[POOL_HALO:, :]
        wsum = tile_sum if wsum is None else jnp.where(group >= gi, tile_sum, wsum)
        src, dst = dst, src
    pos1 = t0 + 1 + lax.broadcasted_iota(jnp.int32, (TM, POOL_WIDTH), 0)
    win_lane = jnp.left_shift(2, group)
    count = jnp.minimum(pos1, win_lane).astype(F32)
    pooled = wsum / count - p_cur
    mixed = _dot(pooled.astype(BF16), poolw_ref[...]) + poolb_ref[...]
    y_pool = mixed * pscale_ref[...]

    a0, a1 = SSD_WIDTH, SSD_WIDTH + SB_WIDTH
    o_ref[...] = (x_ref[...]
                  + _dot(yssd_ref[...].astype(BF16), wout_ref[0:a0, :])
                  + _dot(ysb_ref[...].astype(BF16), wout_ref[a0:a1, :])
                  + _dot(y_pool.astype(BF16), wout_ref[a1:, :]))

    x1 = o_ref[...]
    hb = _rmsnorm(x1, n2_ref[...]).astype(BF16)
    acc = x1
    for lo, hi in ((0, FF_SPLIT), (FF_SPLIT, D_FF)):
        gate = _dot(hb, wg_ref[:, lo:hi])
        up = _dot(hb, wu_ref[:, lo:hi])
        act = (_silu(gate) * up).astype(BF16)
        acc = acc + _dot(act, wd_ref[lo:hi, :])
    if final_norm:
        acc = _rmsnorm(acc, fn_ref[...])
    o_ref[...] = acc


def _pool_out_ffn(x2, y_ssd, y_sb, p, pool_w_bd, pool_b, pool_scale, w_out, norm2_w, w_gate,
                  w_up, w_down, final_w, layer, seq, final_norm):
    nt = x2.shape[0]
    tiles_per_seq = seq // TM_FFN
    halo_per_tile = TM_FFN // POOL_HALO
    row = lambda width: pl.BlockSpec((TM_FFN, width), lambda i: (i, 0))
    spec = lambda shape: _layer_spec(shape, layer, 1, single_buffer=True)
    halo = pl.BlockSpec((POOL_HALO, POOL_WIDTH),
                        lambda i: (jnp.maximum(i * halo_per_tile - 1, 0), 0))
    kern = functools.partial(_pool_out_ffn_kernel, tiles_per_seq=tiles_per_seq,
                             final_norm=final_norm)
    buf_rows = SUBLANES + POOL_HALO + TM_FFN
    return pl.pallas_call(
        kern,
        grid=(nt // TM_FFN,),
        in_specs=[row(D_MODEL), row(SSD_WIDTH), row(SB_WIDTH), row(POOL_WIDTH), halo,
                  spec((POOL_WIDTH, POOL_WIDTH)), spec((1, POOL_WIDTH)), spec((1, POOL_WIDTH)),
                  spec((D_MODEL, D_MODEL)), spec((1, D_MODEL)),
                  spec((D_MODEL, D_FF)), spec((D_MODEL, D_FF)), spec((D_FF, D_MODEL)),
                  pl.BlockSpec((1, D_MODEL), lambda i: (0, 0))],
        out_specs=row(D_MODEL),
        out_shape=jax.ShapeDtypeStruct((nt, D_MODEL), F32),
        scratch_shapes=[pltpu.VMEM((buf_rows, POOL_WIDTH), F32),
                        pltpu.VMEM((buf_rows, POOL_WIDTH), F32)],
        compiler_params=pltpu.CompilerParams(
            dimension_semantics=("arbitrary",), vmem_limit_bytes=VMEM_LIMIT),
        name="pool_out_ffn",
    )(x2, y_ssd, y_sb, p, p, pool_w_bd, pool_b, pool_scale, w_out, norm2_w, w_gate, w_up,
      w_down, final_w)


def _block_diag(pool_w):
    depth, g, c, d = pool_w.shape
    eye = jnp.eye(g, dtype=pool_w.dtype)
    return (pool_w[:, :, :, None, :] * eye[None, :, None, :, None]).reshape(depth, g * c, g * d)


def _rows(v, width=None):
    if width is not None and width > v.shape[-1]:
        v = jnp.pad(v, ((0, 0), (0, width - v.shape[-1])))
    return v[:, None, :]


def kernel(x, norm1_w, w_in, conv_w, conv_b, dt_bias, a_log, d_skip, ssd_norm_w, pool_w, pool_b,
           pool_scale, w_out, norm2_w, w_gate, w_up, w_down, final_norm_w):
    batch, seq, d_model = x.shape
    x2 = x.reshape(batch * seq, d_model)
    w_in = jnp.pad(w_in, ((0, 0), (0, 0), (0, D_IN_PAD - D_IN_PROJ))).astype(BF16)
    w_out, w_gate, w_up, w_down = (w.astype(BF16) for w in (w_out, w_gate, w_up, w_down))
    pool_w_bd = _block_diag(pool_w).astype(BF16)
    norm1_r, norm2_r, conv_b_r = _rows(norm1_w), _rows(norm2_w), _rows(conv_b)
    dt_bias_r, a_log_r = _rows(dt_bias, DT_PAD), _rows(a_log, DT_PAD)
    d_skip_r = _rows(jnp.repeat(d_skip, SSD_HEAD_DIM, axis=1))
    ssd_norm_r = _rows(ssd_norm_w)
    pool_b_r = _rows(pool_b.reshape(DEPTH, POOL_WIDTH))
    pool_scale_r = _rows(pool_scale)
    final_w = final_norm_w.reshape(1, d_model)
    ssd_consts = _ssd_constants()
    for layer in range(DEPTH):
        z, xbc, qkv, p, dt = _norm_inproj(x2, norm1_r, w_in, layer)
        y_ssd = _ssd(z, xbc, dt, conv_w, conv_b_r, dt_bias_r, a_log_r, d_skip_r, ssd_norm_r,
                     ssd_consts, layer, batch, seq)
        y_sb = _sb_attention(qkv, batch, seq)
        x2 = _pool_out_ffn(x2, y_ssd, y_sb, p, pool_w_bd, pool_b_r, pool_scale_r, w_out, norm2_r,
                           w_gate, w_up, w_down, final_w, layer, seq,
                           final_norm=(layer == DEPTH - 1))
    return x2.reshape(batch, seq, d_model)
```

```python
import functools

import jax
import jax.numpy as jnp
from jax import lax
from jax.experimental import pallas as pl
from jax.experimental.pallas import tpu as pltpu

F32 = jnp.float32
BF16 = jnp.bfloat16

D_MODEL = 1024
DEPTH = 4
SSD_WIDTH = 512
SSD_HEADS = 8
SSD_HEAD_DIM = 64
SSD_GROUPS = 2
D_STATE = 128
CONV_WIDTH = 4
CHUNK = 128
CONV_DIM = SSD_WIDTH + 2 * SSD_GROUPS * D_STATE
SB_WIDTH = 256
SB_HEADS = 4
SB_HEAD_DIM = 64
POOL_WINDOWS = (2, 4, 8, 16)
POOL_WIDTH = 256
POOL_GROUP_DIM = 64
D_FF = 2816
EPS = 1e-6
HEAD_DIM_SHIFT = 6
LOG2E = 1.4426950408889634

LANES = 128
SUBLANES = 8
DT_PAD = LANES
VMEM_LIMIT = 56 * 1024 * 1024

TM_PROJ = 1024
TM_FFN = 512
FF_SPLIT = 1536
SB_BLOCK = 256
SSD_BATCH_TILE = 8
POOL_HALO = 16


def _dot(a, b):
    return jnp.dot(a, b, preferred_element_type=F32)


def _softplus(x):
    return jnp.maximum(x, 0.0) + jnp.log1p(jnp.exp(-jnp.abs(x)))


def _silu(x):
    h = 0.5 * x
    return h + h * jnp.tanh(h)


def _split_bf16(x, n):
    parts = []
    r = x
    for i in range(n):
        p = r.astype(BF16)
        parts.append(p)
        if i + 1 < n:
            r = r - p.astype(F32)
    return parts


def _dot_f32_lhs(x, rhs_exact, n):
    parts = _split_bf16(x, n)
    acc = _dot(parts[0], rhs_exact)
    for p in parts[1:]:
        acc = acc + _dot(p, rhs_exact)
    return acc


def _dot_f32_rhs(lhs_exact, x, n):
    parts = _split_bf16(x, n)
    acc = _dot(lhs_exact, parts[0])
    for p in parts[1:]:
        acc = acc + _dot(lhs_exact, p)
    return acc


def _rmsnorm(x, w):
    return x * lax.rsqrt(jnp.mean(x * x, axis=-1, keepdims=True) + EPS) * w


D_IN_PROJ = SSD_WIDTH + CONV_DIM + SSD_HEADS + 3 * SB_WIDTH + POOL_WIDTH
_Z_COLS = (0, SSD_WIDTH)
_XBC_COLS = (SSD_WIDTH, SSD_WIDTH + CONV_DIM)
_TAIL_COLS = (_XBC_COLS[1], D_IN_PROJ)


def _layer_spec(shape, layer, grid_rank, single_buffer=False):
    zeros = (0,) * len(shape)
    index_map = {1: lambda i: (layer,) + zeros, 2: lambda i, j: (layer,) + zeros}[grid_rank]
    kwargs = {"pipeline_mode": pl.Buffered(1)} if single_buffer else {}
    return pl.BlockSpec((None,) + tuple(shape), index_map, **kwargs)


def _norm_inproj_kernel(x_ref, nw_ref, w_ref, z_ref, xbc_ref, qkv_ref, p_ref, dt_ref):
    hb = _rmsnorm(x_ref[...], nw_ref[...]).astype(BF16)
    z_ref[...] = _dot(hb, w_ref[:, _Z_COLS[0]:_Z_COLS[1]])
    xbc_ref[...] = _dot(hb, w_ref[:, _XBC_COLS[0]:_XBC_COLS[1]])
    tail = _dot(hb, w_ref[:, _TAIL_COLS[0]:_TAIL_COLS[1]])
    dt_ref[...] = tail[:, 0:DT_PAD]
    qkv_ref[...] = tail[:, SSD_HEADS:SSD_HEADS + 3 * SB_WIDTH]
    p_ref[...] = tail[:, SSD_HEADS + 3 * SB_WIDTH:]


def _norm_inproj(x2, norm_w, w_in, layer):
    nt = x2.shape[0]
    row = lambda width: pl.BlockSpec((TM_PROJ, width), lambda i: (i, 0))
    widths = [SSD_WIDTH, CONV_DIM, 3 * SB_WIDTH, POOL_WIDTH, DT_PAD]
    return pl.pallas_call(
        _norm_inproj_kernel,
        grid=(nt // TM_PROJ,),
        in_specs=[row(D_MODEL), _layer_spec((1, D_MODEL), layer, 1),
                  _layer_spec((D_MODEL, D_IN_PROJ), layer, 1, single_buffer=True)],
        out_specs=[row(w) for w in widths],
        out_shape=[jax.ShapeDtypeStruct((nt, w), F32) for w in widths],
        compiler_params=pltpu.CompilerParams(
            dimension_semantics=("arbitrary",), vmem_limit_bytes=VMEM_LIMIT),
        name="norm_inproj",
    )(x2, norm_w, w_in)


def _ssd_kernel(z_ref, xbc_ref, dt_ref, convw_ref, convb_ref, dtb_ref, alog_ref, dskip_ref,
                nw_ref, tri_ref, upper_ref, expand_ref, o_ref, ext_ref, state_ref):
    c = pl.program_id(1)

    @pl.when(c == 0)
    def _():
        ext_ref[:, 0:SUBLANES, :] = jnp.zeros((SSD_BATCH_TILE, SUBLANES, CONV_DIM), F32)
        state_ref[...] = jnp.zeros_like(state_ref)

    for e in range(SSD_BATCH_TILE):
        _ssd_chunk(z_ref.at[e], xbc_ref.at[e], dt_ref.at[e], convw_ref, convb_ref, dtb_ref,
                   alog_ref, dskip_ref, nw_ref, tri_ref, upper_ref, expand_ref, o_ref.at[e],
                   ext_ref.at[e], state_ref.at[e])


def _ssd_chunk(z_ref, xbc_ref, dt_ref, convw_ref, convb_ref, dtb_ref, alog_ref, dskip_ref,
               nw_ref, tri_ref, upper_ref, expand_ref, o_ref, ext_ref, state_ref):
    L = CHUNK
    ext_ref[SUBLANES:SUBLANES + L, :] = xbc_ref[...]
    full = ext_ref[...]
    u = convb_ref[...] + convw_ref[CONV_WIDTH - 1:CONV_WIDTH, :] * full[SUBLANES:, :]
    shifted = full
    for i in range(CONV_WIDTH - 2, -1, -1):
        shifted = pltpu.roll(shifted, 1, axis=0)
        u = u + convw_ref[i:i + 1, :] * shifted[SUBLANES:, :]
    ext_ref[0:SUBLANES, :] = ext_ref[L:L + SUBLANES, :]
    act = _silu(u)
    xs = act[:, 0:SSD_WIDTH]
    xs_b = xs.astype(BF16)
    bc = act[:, SSD_WIDTH:CONV_DIM].astype(BF16)

    dt = _softplus(dt_ref[...] + dtb_ref[...])
    a = -jnp.exp(alog_ref[...])
    d_a = dt * a

    row = lax.broadcasted_iota(jnp.int32, (L, L), 0)
    col = lax.broadcasted_iota(jnp.int32, (L, L), 1)
    causal = col <= row
    acum = _dot_f32_rhs(tri_ref[...], d_a, 3)
    d_a_t = jnp.transpose(d_a)[0:2 * SUBLANES, :]
    acum_t = _dot_f32_lhs(d_a_t, upper_ref[...], 3)
    dt_t = jnp.transpose(dt)[0:2 * SUBLANES, :]

    total = acum[L - 1:L, :]
    eac = jnp.exp(acum)
    f_state = dt * jnp.exp(total - acum)

    expand = expand_ref[...]
    eac_x = _dot_f32_lhs(eac, expand, 2)
    f_state_x = _dot_f32_lhs(f_state, expand, 2)

    lane = lax.broadcasted_iota(jnp.int32, (L, LANES), 1)
    low_half = lane < SSD_HEAD_DIM

    y_parts = []
    for g in range(SSD_GROUPS):
        b_g = bc[:, g * D_STATE:(g + 1) * D_STATE]
        c_off = SSD_GROUPS * D_STATE
        c_g = bc[:, c_off + g * D_STATE:c_off + (g + 1) * D_STATE]
        cb = lax.dot_general(c_g, b_g, (((1,), (1,)), ((), ())), preferred_element_type=F32)
        gw = SSD_WIDTH // SSD_GROUPS
        st = state_ref[g]
        y_off = _dot(c_g, st.astype(BF16)) * eac_x[:, g * gw:(g + 1) * gw]
        for j in range(2):
            lo = g * gw + j * LANES
            x_pair = xs_b[:, lo:lo + LANES]
            y_pair = None
            for k in range(2):
                h = (lo // SSD_HEAD_DIM) + k
                seg = acum[:, h:h + 1] - acum_t[h:h + 1, :]
                decay = jnp.exp(jnp.where(causal, seg, -jnp.inf))
                m = (cb * decay * dt_t[h:h + 1, :]).astype(BF16)
                x_h = jnp.where(low_half if k == 0 else jnp.logical_not(low_half), x_pair, 0)
                contrib = _dot(m, x_h)
                y_pair = contrib if y_pair is None else y_pair + contrib
            y_parts.append(y_pair + y_off[:, j * LANES:(j + 1) * LANES])
        xd = (xs[:, g * gw:(g + 1) * gw] * f_state_x[:, g * gw:(g + 1) * gw]).astype(BF16)
        s_add = lax.dot_general(b_g, xd, (((0,), (0,)), ((), ())), preferred_element_type=F32)
        state_ref[g] = st * eac_x[L - 1:L, g * gw:(g + 1) * gw] + s_add

    y = jnp.concatenate(y_parts, axis=1) + xs * dskip_ref[...]
    y = y * _silu(z_ref[...])
    o_ref[...] = _rmsnorm(y, nw_ref[...])


def _ssd(z, xbc, dt, conv_w, conv_b, dt_bias, a_log, d_skip_x, norm_w, consts, layer, batch,
         seq):
    nc = seq // CHUNK
    as_seq = lambda t: t.reshape(batch, seq, t.shape[-1])
    row = lambda width: pl.BlockSpec((SSD_BATCH_TILE, CHUNK, width), lambda b, c: (b, c, 0))
    const = lambda shape: pl.BlockSpec(shape, lambda b, c: (0, 0))
    spec = lambda shape: _layer_spec(shape, layer, 2)
    tri, upper, expand = consts
    return pl.pallas_call(
        _ssd_kernel,
        grid=(batch // SSD_BATCH_TILE, nc),
        in_specs=[row(SSD_WIDTH), row(CONV_DIM), row(DT_PAD),
                  spec((CONV_WIDTH, CONV_DIM)), spec((1, CONV_DIM)), spec((1, DT_PAD)),
                  spec((1, DT_PAD)), spec((1, SSD_WIDTH)), spec((1, SSD_WIDTH)),
                  const((CHUNK, CHUNK)), const((CHUNK, CHUNK)), const((DT_PAD, SSD_WIDTH))],
        out_specs=row(SSD_WIDTH),
        out_shape=jax.ShapeDtypeStruct((batch, seq, SSD_WIDTH), F32),
        scratch_shapes=[pltpu.VMEM((SSD_BATCH_TILE, CHUNK + SUBLANES, CONV_DIM), F32),
                        pltpu.VMEM((SSD_BATCH_TILE, SSD_GROUPS, D_STATE,
                                    SSD_WIDTH // SSD_GROUPS), F32)],
        compiler_params=pltpu.CompilerParams(
            dimension_semantics=("arbitrary", "arbitrary"), vmem_limit_bytes=VMEM_LIMIT),
        name="ssd_mixer",
    )(as_seq(z), as_seq(xbc), as_seq(dt), conv_w, conv_b, dt_bias, a_log, d_skip_x, norm_w, tri,
      upper, expand).reshape(batch * seq, SSD_WIDTH)


def _ssd_constants():
    idx = jnp.arange(CHUNK)
    tri = (idx[None, :] <= idx[:, None]).astype(BF16)
    upper = (idx[:, None] <= idx[None, :]).astype(BF16)
    expand = (jnp.arange(SSD_WIDTH)[None, :] // SSD_HEAD_DIM
              == jnp.arange(DT_PAD)[:, None]).astype(BF16)
    return tri, upper, expand


def _sb_kernel(q_ref, k_ref, v_ref, o_ref, kb_ref, vbm_ref, qs_ref, acc_ref, carry_ref):
    T = SB_BLOCK
    H = SB_HEADS
    U = 2 * H
    si = pl.program_id(1)
    head_of_lane = jnp.right_shift(lax.broadcasted_iota(jnp.int32, (1, SB_WIDTH), 1),
                                   HEAD_DIM_SHIFT)

    @pl.when(si == 0)
    def _():
        kb_ref[...] = k_ref[...].astype(BF16)
        v = v_ref[...]
        for h in range(H):
            vbm_ref[h] = jnp.where(head_of_lane == h, v, 0.0).astype(BF16)

    for u in range(U):
        q = q_ref[(u // H) * T:(u // H + 1) * T, :] * (SB_HEAD_DIM ** -0.5)
        qs_ref[u * T:(u + 1) * T, :] = jnp.where(head_of_lane == u % H, q, 0.0).astype(BF16)

    row = lax.broadcasted_iota(jnp.int32, (T, T), 0)
    col = lax.broadcasted_iota(jnp.int32, (T, T), 1)
    before = col < row
    tri = (row > col).astype(BF16)

    def scores(u0, n_units, j):
        kb = kb_ref[pl.ds(pl.multiple_of(j * T, T), T), :]
        return lax.dot_general(qs_ref[u0 * T:(u0 + n_units) * T, :], kb,
                               (((1,), (1,)), ((), ())), preferred_element_type=F32)

    def weights(zs, carries, diagonal):
        ws = []
        new_carries = []
        for i in range(zs.shape[0] // T):
            z = zs[i * T:(i + 1) * T, :]
            e = jnp.exp2(jnp.abs(z) * (-LOG2E))
            sp = jnp.maximum(z, 0.0) + jnp.log(1.0 + e)
            if diagonal:
                sp = jnp.where(before, sp, 0.0)
            arg = z - sp - _dot(sp.astype(BF16), tri)
            if carries is not None:
                arg = arg - jnp.sum(carries[i], axis=1, keepdims=True)
            w = jnp.exp2(arg * LOG2E)
            if diagonal:
                w = jnp.where(before, w, 0.0)
            ws.append(w.astype(BF16))
            part = sp[:, 0:LANES] + sp[:, LANES:2 * LANES]
            new_carries.append(part if carries is None else carries[i] + part)
        return ws, new_carries

    def values(ws, j):
        start = pl.multiple_of(j * T, T)
        n_q = len(ws) // H
        pv = None
        for h in range(H):
            lhs = ws[h] if n_q == 1 else jnp.concatenate([ws[h], ws[H + h]], axis=0)
            t = _dot(lhs, vbm_ref[h, pl.ds(start, T), :])
            pv = t if pv is None else pv + t
        return pv

    ja = 2 * si
    ws_a, carries_a = weights(scores(0, H, ja), None, True)
    ws_b1, carries_b = weights(scores(H, H, ja + 1), None, True)
    ws_b0, carries_b = weights(scores(H, H, ja), carries_b, False)
    acc_ref[0:T, :] = values(ws_a, ja)
    acc_ref[T:2 * T, :] = values(ws_b1, ja + 1) + values(ws_b0, ja)
    for u, c in enumerate(carries_a + carries_b):
        carry_ref[u] = c

    def body(it, _):
        j = ja - 1 - 2 * it
        ws1, carries = weights(scores(0, U, j), [carry_ref[u] for u in range(U)], False)
        ws0, carries = weights(scores(0, U, j - 1), carries, False)
        acc_ref[...] += values(ws1, j) + values(ws0, j - 1)
        for u in range(U):
            carry_ref[u] = carries[u]
        return 0

    lax.fori_loop(0, si, body, 0)
    o_ref[...] = acc_ref[...]


def _sb_attention(qkv, batch, seq):
    nt = qkv.shape[0]
    rows = 2 * SB_BLOCK
    ns = seq // rows
    return pl.pallas_call(
        _sb_kernel,
        grid=(batch, ns),
        in_specs=[pl.BlockSpec((rows, SB_WIDTH), lambda b, i: (b * ns + i, 0)),
                  pl.BlockSpec((seq, SB_WIDTH), lambda b, i: (b, 1)),
                  pl.BlockSpec((seq, SB_WIDTH), lambda b, i: (b, 2))],
        out_specs=pl.BlockSpec((rows, SB_WIDTH), lambda b, i: (b * ns + i, 0)),
        out_shape=jax.ShapeDtypeStruct((nt, SB_WIDTH), F32),
        scratch_shapes=[pltpu.VMEM((seq, SB_WIDTH), BF16),
                        pltpu.VMEM((SB_HEADS, seq, SB_WIDTH), BF16),
                        pltpu.VMEM((2 * SB_HEADS * SB_BLOCK, SB_WIDTH), BF16),
                        pltpu.VMEM((rows, SB_WIDTH), F32),
                        pltpu.VMEM((2 * SB_HEADS, SB_BLOCK, LANES), F32)],
        compiler_params=pltpu.CompilerParams(
            dimension_semantics=("arbitrary", "arbitrary"), vmem_limit_bytes=VMEM_LIMIT),
        name="sb_attention",
    )(qkv, qkv, qkv)


def _pool_out_ffn_kernel(x_ref, yssd_ref, ysb_ref, p_ref, halo_ref, poolw_ref, poolb_ref,
                         pscale_ref, wout_ref, n2_ref, wg_ref, wu_ref, wd_ref, fn_ref,
                         o_ref, buf_a, buf_b, *, tiles_per_seq, final_norm):
    TM = TM_FFN
    i = pl.program_id(0)
    t0 = (i % tiles_per_seq) * TM

    top = SUBLANES + POOL_HALO
    p_cur = p_ref[...]
    zeros8 = jnp.zeros((SUBLANES, POOL_WIDTH), F32)
    buf_a[0:SUBLANES, :] = zeros8
    buf_b[0:SUBLANES, :] = zeros8
    buf_a[SUBLANES:top, :] = jnp.where(t0 == 0, 0.0, halo_ref[...])
    buf_a[top:top + TM, :] = p_cur
    lane = lax.broadcasted_iota(jnp.int32, (TM, POOL_WIDTH), 1)
    group = jnp.right_shift(lane, HEAD_DIM_SHIFT)
    n_rows = TM + POOL_HALO
    wsum = None
    src, dst = buf_a, buf_b
    for gi, win in enumerate(POOL_WINDOWS):
        shift = win // 2
        s = src[SUBLANES:SUBLANES + n_rows, :] + src[SUBLANES - shift:SUBLANES - shift + n_rows, :]
        dst[SUBLANES:SUBLANES + n_rows, :] = s
        tile_sum = s[POOL_HALO:, :]
        wsum = tile_sum if wsum is None else jnp.where(group >= gi, tile_sum, wsum)
        src, dst = dst, src
    pos1 = t0 + 1 + lax.broadcasted_iota(jnp.int32, (TM, POOL_WIDTH), 0)
    win_lane = jnp.left_shift(2, group)
    count = jnp.minimum(pos1, win_lane).astype(F32)
    pooled = wsum / count - p_cur
    mixed = _dot(pooled.astype(BF16), poolw_ref[...]) + poolb_ref[...]
    y_pool = mixed * pscale_ref[...]

    a0, a1 = SSD_WIDTH, SSD_WIDTH + SB_WIDTH
    o_ref[...] = (x_ref[...]
                  + _dot(yssd_ref[...].astype(BF16), wout_ref[0:a0, :])
                  + _dot(ysb_ref[...].astype(BF16), wout_ref[a0:a1, :])
                  + _dot(y_pool.astype(BF16), wout_ref[a1:, :]))

    x1 = o_ref[...]
    hb = _rmsnorm(x1, n2_ref[...]).astype(BF16)
    acc = x1
    for lo, hi in ((0, FF_SPLIT), (FF_SPLIT, D_FF)):
        gate = _dot(hb, wg_ref[:, lo:hi])
        up = _dot(hb, wu_ref[:, lo:hi])
        act = (_silu(gate) * up).astype(BF16)
        acc = acc + _dot(act, wd_ref[lo:hi, :])
    if final_norm:
        acc = _rmsnorm(acc, fn_ref[...])
    o_ref[...] = acc


def _pool_out_ffn(x2, y_ssd, y_sb, p, pool_w_bd, pool_b, pool_scale, w_out, norm2_w, w_gate,
                  w_up, w_down, final_w, layer, seq, final_norm):
    nt = x2.shape[0]
    tiles_per_seq = seq // TM_FFN
    halo_per_tile = TM_FFN // POOL_HALO
    row = lambda width: pl.BlockSpec((TM_FFN, width), lambda i: (i, 0))
    spec = lambda shape: _layer_spec(shape, layer, 1, single_buffer=True)
    halo = pl.BlockSpec((POOL_HALO, POOL_WIDTH),
                        lambda i: (jnp.maximum(i * halo_per_tile - 1, 0), 0))
    kern = functools.partial(_pool_out_ffn_kernel, tiles_per_seq=tiles_per_seq,
                             final_norm=final_norm)
    buf_rows = SUBLANES + POOL_HALO + TM_FFN
    return pl.pallas_call(
        kern,
        grid=(nt // TM_FFN,),
        in_specs=[row(D_MODEL), row(SSD_WIDTH), row(SB_WIDTH), row(POOL_WIDTH), halo,
                  spec((POOL_WIDTH, POOL_WIDTH)), spec((1, POOL_WIDTH)), spec((1, POOL_WIDTH)),
                  spec((D_MODEL, D_MODEL)), spec((1, D_MODEL)),
                  spec((D_MODEL, D_FF)), spec((D_MODEL, D_FF)), spec((D_FF, D_MODEL)),
                  pl.BlockSpec((1, D_MODEL), lambda i: (0, 0))],
        out_specs=row(D_MODEL),
        out_shape=jax.ShapeDtypeStruct((nt, D_MODEL), F32),
        scratch_shapes=[pltpu.VMEM((buf_rows, POOL_WIDTH), F32),
                        pltpu.VMEM((buf_rows, POOL_WIDTH), F32)],
        compiler_params=pltpu.CompilerParams(
            dimension_semantics=("arbitrary",), vmem_limit_bytes=VMEM_LIMIT,
            allow_input_fusion=[n in (8, 10, 11, 12) for n in range(14)]),
        name="pool_out_ffn",
    )(x2, y_ssd, y_sb, p, p, pool_w_bd, pool_b, pool_scale, w_out, norm2_w, w_gate, w_up,
      w_down, final_w)


def _block_diag(pool_w):
    depth, g, c, d = pool_w.shape
    eye = jnp.eye(g, dtype=pool_w.dtype)
    return (pool_w[:, :, :, None, :] * eye[None, :, None, :, None]).reshape(depth, g * c, g * d)


def _rows(v, width=None):
    if width is not None and width > v.shape[-1]:
        v = jnp.pad(v, ((0, 0), (0, width - v.shape[-1])))
    return v[:, None, :]


def kernel(x, norm1_w, w_in, conv_w, conv_b, dt_bias, a_log, d_skip, ssd_norm_w, pool_w, pool_b,
           pool_scale, w_out, norm2_w, w_gate, w_up, w_down, final_norm_w):
    batch, seq, d_model = x.shape
    x2 = x.reshape(batch * seq, d_model)
    w_in = w_in.astype(BF16)
    w_out, w_gate, w_up, w_down = (w.astype(BF16) for w in (w_out, w_gate, w_up, w_down))
    pool_w_bd = _block_diag(pool_w).astype(BF16)
    norm1_r, norm2_r, conv_b_r = _rows(norm1_w), _rows(norm2_w), _rows(conv_b)
    dt_bias_r, a_log_r = _rows(dt_bias, DT_PAD), _rows(a_log, DT_PAD)
    d_skip_r = _rows(jnp.repeat(d_skip, SSD_HEAD_DIM, axis=1))
    ssd_norm_r = _rows(ssd_norm_w)
    pool_b_r = _rows(pool_b.reshape(DEPTH, POOL_WIDTH))
    pool_scale_r = _rows(pool_scale)
    final_w = final_norm_w.reshape(1, d_model)
    ssd_consts = _ssd_constants()
    for layer in range(DEPTH):
        z, xbc, qkv, p, dt = _norm_inproj(x2, norm1_r, w_in, layer)
        y_ssd = _ssd(z, xbc, dt, conv_w, conv_b_r, dt_bias_r, a_log_r, d_skip_r, ssd_norm_r,
                     ssd_consts, layer, batch, seq)
        y_sb = _sb_attention(qkv, batch, seq)
        x2 = _pool_out_ffn(x2, y_ssd, y_sb, p, pool_w_bd, pool_b_r, pool_scale_r, w_out, norm2_r,
                           w_gate, w_up, w_down, final_w, layer, seq,
                           final_norm=(layer == DEPTH - 1))
    return x2.reshape(batch, seq, d_model)
```
